```python
import math
import jax, jax.numpy as jnp
from jax import lax
import numpy as np

D_MODEL = 1024
BATCH = 2
SEQ = 16384
DEPTH = 4

N_MEM = 256
EPS = 1e-5
N_EVEN = (DEPTH + 1) // 2
N_ODD = DEPTH // 2

SSD_HEADS = 16
SSD_HEAD_DIM = 64
SSD_INNER = SSD_HEADS * SSD_HEAD_DIM
SSD_GROUPS = 2
SSD_STATE = 64
SSD_CONV = 4
SSD_CHUNK = 128
SSD_XBC = SSD_INNER + 2 * SSD_GROUPS * SSD_STATE

GLA_HEADS = 4
GLA_KEY = D_MODEL // 2
GLA_VAL = D_MODEL
GLA_HK = GLA_KEY // GLA_HEADS
GLA_HV = GLA_VAL // GLA_HEADS
GLA_RANK = 16
GLA_TAU = 16.0
GLA_CHUNK = 64

IN_SPLITS = (SSD_INNER, SSD_XBC, SSD_HEADS, GLA_KEY, GLA_KEY, GLA_VAL, GLA_RANK, GLA_VAL)
IN_WIDTH = sum(IN_SPLITS)
MIX_WIDTH = SSD_INNER + GLA_VAL

DIFF_HEADS = 8
DIFF_HEAD_DIM = 64
DIFF_V_DIM = 2 * DIFF_HEAD_DIM
DIFF_QK = DIFF_HEADS * 2 * DIFF_HEAD_DIM
DIFF_QKV = 2 * DIFF_QK + DIFF_HEADS * DIFF_V_DIM
Q_BLOCK = 128

X_HEADS = 4
X_HEAD_DIM = D_MODEL // X_HEADS

D_FF = 4 * D_MODEL

kernel_name = "hybrid_ssd_gla_diffattn_trunk"


def rmsnorm(x, w):
    xf = x.astype(jnp.float32)
    y = xf * lax.rsqrt(jnp.mean(xf * xf, axis=-1, keepdims=True) + EPS)
    return (y * w.astype(jnp.float32)).astype(x.dtype)


def causal_dwconv(x, w, b):
    k = w.shape[0]
    y = lax.conv_general_dilated(
        x, w[:, None, :].astype(x.dtype), window_strides=(1,), padding=[(k - 1, 0)],
        dimension_numbers=("NWC", "WIO", "NWC"), feature_group_count=x.shape[-1])
    return y + b.astype(x.dtype)


def to_chunks(t, size):
    b, l = t.shape[:2]
    return jnp.moveaxis(t.reshape(b, l // size, size, *t.shape[2:]), 1, 0)


def from_chunks(t):
    nc, b, q = t.shape[:3]
    return jnp.moveaxis(t, 0, 1).reshape(b, nc * q, *t.shape[3:])


def ssd_chunked(x, dt, a, bmat, cmat):
    f32 = jnp.float32
    b, l, h, p = x.shape
    g, n = bmat.shape[2:]
    e = h // g
    xdt = (x.astype(f32) * dt.astype(f32)[..., None]).reshape(b, l, g, e, p)
    loga = (dt.astype(f32) * a.astype(f32)).reshape(b, l, g, e)
    mask = jnp.tril(jnp.ones((SSD_CHUNK, SSD_CHUNK), dtype=bool))[None, :, :, None, None]

    def step(state, inp):
        xc, ac, bc, cc = inp
        cum = jnp.cumsum(ac, axis=1)
        seg = cum[:, :, None] - cum[:, None, :]
        decay = jnp.exp(jnp.where(mask, seg, -jnp.inf))
        scores = jnp.einsum("bqgn,bsgn->bqsg", cc, bc)
        y = jnp.einsum("bqsg,bqsge,bsgep->bqgep", scores, decay, xc)
        y = y + jnp.einsum("bqgn,bgepn->bqgep", cc, state) * jnp.exp(cum)[..., None]
        last = cum[:, -1]
        wts = jnp.exp(last[:, None] - cum)
        state = state * jnp.exp(last)[..., None, None] + jnp.einsum("bsge,bsgep,bsgn->bgepn", wts, xc, bc)
        return state, y

    state0 = jnp.zeros((b, g, e, p, n), f32)
    inputs = (to_chunks(xdt, SSD_CHUNK), to_chunks(loga, SSD_CHUNK),
              to_chunks(bmat.astype(f32), SSD_CHUNK), to_chunks(cmat.astype(f32), SSD_CHUNK))
    _, ys = lax.scan(step, state0, inputs)
    return from_chunks(ys).reshape(b, l, h, p).astype(x.dtype)


def gla_chunked(q, k, v, logg):
    f32 = jnp.float32
    b, l, h, dk = q.shape
    dv = v.shape[-1]
    mask = jnp.tril(jnp.ones((GLA_CHUNK, GLA_CHUNK), dtype=bool))[None, :, :, None, None]

    def step(s_state, inp):
        qc, kc, vc, gc = inp
        cum = jnp.cumsum(gc, axis=1)
        seg = cum[:, :, None] - cum[:, None, :]
        decay = jnp.exp(jnp.where(mask, seg, -jnp.inf))
        scores = jnp.einsum("bqhk,bshk,bqshk->bqsh", qc, kc, decay)
        o = jnp.einsum("bqsh,bshv->bqhv", scores, vc)
        o = o + jnp.einsum("bqhk,bhkv->bqhv", qc * jnp.exp(cum), s_state)
        last = cum[:, -1]
        s_state = s_state * jnp.exp(last)[..., None] + jnp.einsum(
            "bshk,bshv->bhkv", kc * jnp.exp(last[:, None] - cum), vc)
        return s_state, o

    s0 = jnp.zeros((b, h, dk, dv), f32)
    inputs = tuple(to_chunks(t.astype(f32), GLA_CHUNK) for t in (q, k, v, logg))
    _, os_ = lax.scan(step, s0, inputs)
    return from_chunks(os_).astype(v.dtype)


def ssd_gla_mixer(h, w_in, conv_w, conv_b, dt_bias, a_log, d_skip, ssd_norm, gla_w2, gla_b, gla_norm, w_out):
    b, l, _ = h.shape
    proj = h @ w_in
    z, xbc, dt, q, k, v, glr, r = jnp.split(proj, list(np.cumsum(IN_SPLITS)[:-1]), axis=-1)

    xbc = jax.nn.silu(causal_dwconv(xbc, conv_w, conv_b))
    xs, bm, cm = jnp.split(xbc, [SSD_INNER, SSD_INNER + SSD_GROUPS * SSD_STATE], axis=-1)
    xs = xs.reshape(b, l, SSD_HEADS, SSD_HEAD_DIM)
    bm = bm.reshape(b, l, SSD_GROUPS, SSD_STATE)
    cm = cm.reshape(b, l, SSD_GROUPS, SSD_STATE)
    dt = jax.nn.softplus((dt + dt_bias).astype(jnp.float32))
    a = -jnp.exp(a_log.astype(jnp.float32))
    y = ssd_chunked(xs, dt, a, bm, cm) + d_skip[:, None] * xs
    y = y.reshape(b, l, SSD_INNER) * jax.nn.silu(z)
    y = rmsnorm(y.reshape(b, l, SSD_GROUPS, SSD_INNER // SSD_GROUPS),
                ssd_norm.reshape(SSD_GROUPS, SSD_INNER // SSD_GROUPS)).reshape(b, l, SSD_INNER)

    q = q.reshape(b, l, GLA_HEADS, GLA_HK) * (GLA_HK ** -0.5)
    k = k.reshape(b, l, GLA_HEADS, GLA_HK)
    v = v.reshape(b, l, GLA_HEADS, GLA_HV)
    logg = jax.nn.log_sigmoid((glr @ gla_w2 + gla_b).astype(jnp.float32)) / GLA_TAU
    logg = logg.reshape(b, l, GLA_HEADS, GLA_HK)
    o = gla_chunked(q, k, v, logg)
    o = rmsnorm(o, gla_norm).reshape(b, l, GLA_VAL) * jax.nn.silu(r)

    return jnp.concatenate([y, o], axis=-1) @ w_out


def alibi_slopes(n):
    start = 2.0 ** (-8.0 / n)
    return np.array([start ** (i + 1) for i in range(n)], dtype=np.float32)


def diff_attention(h, w_qkv, lam_q1, lam_k1, lam_q2, lam_k2, subln, w_o, lambda_init):
    f32 = jnp.float32
    b, l, _ = h.shape
    qkv = h @ w_qkv
    q, k, v = jnp.split(qkv, [DIFF_QK, 2 * DIFF_QK], axis=-1)
    q = q.reshape(b, l, DIFF_HEADS, 2, DIFF_HEAD_DIM) * (DIFF_HEAD_DIM ** -0.5)
    k = k.reshape(b, l, DIFF_HEADS, 2, DIFF_HEAD_DIM)
    v = v.reshape(b, l, DIFF_HEADS, DIFF_V_DIM)
    lam = (jnp.exp(jnp.sum(lam_q1.astype(f32) * lam_k1.astype(f32)))
           - jnp.exp(jnp.sum(lam_q2.astype(f32) * lam_k2.astype(f32))) + lambda_init)
    slopes = jnp.asarray(alibi_slopes(DIFF_HEADS))[:, None, None]
    nb = l // Q_BLOCK
    qb = jnp.moveaxis(q.reshape(b, nb, Q_BLOCK, DIFF_HEADS, 2, DIFF_HEAD_DIM), 1, 0)
    kpos = jnp.arange(l)

    def block(args):
        qblk, i = args
        qpos = i * Q_BLOCK + jnp.arange(Q_BLOCK)
        dist = (qpos[:, None] - kpos[None, :]).astype(f32)
        bias = jnp.where(dist >= 0, -slopes * dist, -jnp.inf)
        s = jnp.einsum("bqhmd,bkhmd->bhmqk", qblk, k).astype(f32) + bias[None, :, None]
        p = jax.nn.softmax(s, axis=-1)
        attn = p[:, :, 0] - lam * p[:, :, 1]
        return jnp.einsum("bhqk,bkhv->bqhv", attn.astype(v.dtype), v)

    o = from_chunks(lax.map(block, (qb, jnp.arange(nb))))
    o = rmsnorm(o, subln) * (1.0 - lambda_init)
    return o.reshape(b, l, DIFF_HEADS * DIFF_V_DIM) @ w_o


def cross_attention(h, mem_n, wq, wkv, wo):
    b, l, _ = h.shape
    m = mem_n.shape[1]
    q = (h @ wq).reshape(b, l, X_HEADS, X_HEAD_DIM)
    k, v = jnp.split(mem_n @ wkv, 2, axis=-1)
    k = k.reshape(b, m, X_HEADS, X_HEAD_DIM)
    v = v.reshape(b, m, X_HEADS, X_HEAD_DIM)
    s = jnp.einsum("bqhd,bmhd->bhqm", q, k).astype(jnp.float32) * (X_HEAD_DIM ** -0.5)
    p = jax.nn.softmax(s, axis=-1)
    o = jnp.einsum("bhqm,bmhd->bqhd", p.astype(v.dtype), v)
    return o.reshape(b, l, D_MODEL) @ wo


def sq_relu_mlp(h, w1, w2):
    return jnp.square(jax.nn.relu(h @ w1)) @ w2


def setup_inputs(seed: int = 0) -> dict:
    key = jax.random.key(seed)
    ks = iter(jax.random.split(key, 40))

    def nrm(shape, scale):
        return jax.random.normal(next(ks), shape, jnp.float32) * scale

    def gain(shape):
        return 1.0 + nrm(shape, 0.02)

    x = nrm((BATCH, SEQ, D_MODEL), 1.0)
    mem = nrm((BATCH, N_MEM, D_MODEL), 1.0)
    ev_norm = gain((N_EVEN, D_MODEL))
    ev_w_in = nrm((N_EVEN, D_MODEL, IN_WIDTH), D_MODEL ** -0.5)
    ev_conv_w = nrm((N_EVEN, SSD_CONV, SSD_XBC), SSD_CONV ** -0.5)
    ev_conv_b = nrm((N_EVEN, SSD_XBC), 0.01)
    dt0 = jnp.exp(jax.random.uniform(next(ks), (N_EVEN, SSD_HEADS), jnp.float32,
                                     minval=math.log(1e-3), maxval=math.log(1e-1)))
    ev_dt_bias = dt0 + jnp.log(-jnp.expm1(-dt0))
    ev_a_log = jnp.log(jax.random.uniform(next(ks), (N_EVEN, SSD_HEADS), jnp.float32, minval=1.0, maxval=16.0))
    ev_d_skip = gain((N_EVEN, SSD_HEADS))
    ev_ssd_norm = gain((N_EVEN, SSD_INNER))
    ev_gla_w2 = nrm((N_EVEN, GLA_RANK, GLA_KEY), GLA_RANK ** -0.5)
    ev_gla_b = nrm((N_EVEN, GLA_KEY), 0.1)
    ev_gla_norm = gain((N_EVEN, GLA_HV))
    ev_w_out = nrm((N_EVEN, MIX_WIDTH, D_MODEL), MIX_WIDTH ** -0.5)
    od_norm = gain((N_ODD, D_MODEL))
    od_w_qkv = nrm((N_ODD, D_MODEL, DIFF_QKV), D_MODEL ** -0.5)
    od_lam_q1 = nrm((N_ODD, DIFF_HEAD_DIM), 0.1)
    od_lam_k1 = nrm((N_ODD, DIFF_HEAD_DIM), 0.1)
    od_lam_q2 = nrm((N_ODD, DIFF_HEAD_DIM), 0.1)
    od_lam_k2 = nrm((N_ODD, DIFF_HEAD_DIM), 0.1)
    od_subln = gain((N_ODD, DIFF_V_DIM))
    od_w_o = nrm((N_ODD, DIFF_HEADS * DIFF_V_DIM, D_MODEL), (DIFF_HEADS * DIFF_V_DIM) ** -0.5)
    xa_norm = gain((DEPTH, D_MODEL))
    xa_mem_norm = gain((DEPTH, D_MODEL))
    xa_wq = nrm((DEPTH, D_MODEL, D_MODEL), D_MODEL ** -0.5)
    xa_wkv = nrm((DEPTH, D_MODEL, 2 * D_MODEL), D_MODEL ** -0.5)
    xa_wo = nrm((DEPTH, D_MODEL, D_MODEL), D_MODEL ** -0.5)
    mlp_norm = gain((DEPTH, D_MODEL))
    mlp_w1 = nrm((DEPTH, D_MODEL, D_FF), D_MODEL ** -0.5)
    mlp_w2 = nrm((DEPTH, D_FF, D_MODEL), D_FF ** -0.5)
    final_norm = gain((D_MODEL,))
    return {"x": x, "mem": mem,
            "ev_norm": ev_norm, "ev_w_in": ev_w_in, "ev_conv_w": ev_conv_w, "ev_conv_b": ev_conv_b,
            "ev_dt_bias": ev_dt_bias, "ev_a_log": ev_a_log, "ev_d_skip": ev_d_skip, "ev_ssd_norm": ev_ssd_norm,
            "ev_gla_w2": ev_gla_w2, "ev_gla_b": ev_gla_b, "ev_gla_norm": ev_gla_norm, "ev_w_out": ev_w_out,
            "od_norm": od_norm, "od_w_qkv": od_w_qkv, "od_lam_q1": od_lam_q1, "od_lam_k1": od_lam_k1,
            "od_lam_q2": od_lam_q2, "od_lam_k2": od_lam_k2, "od_subln": od_subln, "od_w_o": od_w_o,
            "xa_norm": xa_norm, "xa_mem_norm": xa_mem_norm, "xa_wq": xa_wq, "xa_wkv": xa_wkv, "xa_wo": xa_wo,
            "mlp_norm": mlp_norm, "mlp_w1": mlp_w1, "mlp_w2": mlp_w2, "final_norm": final_norm}


def reference(x, mem, ev_norm, ev_w_in, ev_conv_w, ev_conv_b, ev_dt_bias, ev_a_log, ev_d_skip, ev_ssd_norm,
              ev_gla_w2, ev_gla_b, ev_gla_norm, ev_w_out, od_norm, od_w_qkv, od_lam_q1, od_lam_k1, od_lam_q2,
              od_lam_k2, od_subln, od_w_o, xa_norm, xa_mem_norm, xa_wq, xa_wkv, xa_wo, mlp_norm, mlp_w1,
              mlp_w2, final_norm):
    for layer in range(DEPTH):
        i = layer // 2
        if layer % 2 == 0:
            x = x + ssd_gla_mixer(rmsnorm(x, ev_norm[i]), ev_w_in[i], ev_conv_w[i], ev_conv_b[i],
                                  ev_dt_bias[i], ev_a_log[i], ev_d_skip[i], ev_ssd_norm[i],
                                  ev_gla_w2[i], ev_gla_b[i], ev_gla_norm[i], ev_w_out[i])
        else:
            lambda_init = 0.8 - 0.6 * math.exp(-0.3 * layer)
            x = x + diff_attention(rmsnorm(x, od_norm[i]), od_w_qkv[i], od_lam_q1[i], od_lam_k1[i],
                                   od_lam_q2[i], od_lam_k2[i], od_subln[i], od_w_o[i], lambda_init)
        x = x + cross_attention(rmsnorm(x, xa_norm[layer]), rmsnorm(mem, xa_mem_norm[layer]),
                                xa_wq[layer], xa_wkv[layer], xa_wo[layer])
        x = x + sq_relu_mlp(rmsnorm(x, mlp_norm[layer]), mlp_w1[layer], mlp_w2[layer])
    return rmsnorm(x, final_norm)
```

```python
import functools
import math

import numpy as np
import jax
import jax.numpy as jnp
from jax import lax
from jax.experimental import pallas as pl
from jax.experimental.pallas import tpu as pltpu

F32 = jnp.float32
BF16 = jnp.bfloat16
HIGHEST = lax.Precision.HIGHEST
EPS = 1e-5
NEG_BIG = -1e30

D_MODEL = 1024
N_LAYERS = 4

SSD_HEADS = 16
SSD_HEAD_DIM = 64
SSD_INNER = SSD_HEADS * SSD_HEAD_DIM
SSD_GROUPS = 2
SSD_STATE = 64
SSD_CONV = 4
SSD_CHUNK = 128
SSD_XBC = SSD_INNER + 2 * SSD_GROUPS * SSD_STATE
SSD_GROUP_WIDTH = SSD_INNER // SSD_GROUPS
SSD_GROUP_HEADS = SSD_HEADS // SSD_GROUPS

GLA_HEADS = 4
GLA_KEY = 512
GLA_VAL = 1024
GLA_HK = GLA_KEY // GLA_HEADS
GLA_HV = GLA_VAL // GLA_HEADS
GLA_RANK = 16
GLA_TAU = 16.0
GLA_CHUNK = 64
GLA_SUB = 16

DIFF_HEADS = 8
DIFF_HEAD_DIM = 64
DIFF_V_DIM = 2 * DIFF_HEAD_DIM
DIFF_QK = DIFF_HEADS * 2 * DIFF_HEAD_DIM

X_HEADS = 4
X_HEAD_DIM = D_MODEL // X_HEADS
D_FF = 4 * D_MODEL

LANES = 128
SUBLANES = 8
VMEM_LIMIT = 56 * 1024 * 1024
TOKEN_TILE = 512
ATTN_TILE = 512
MATMUL_COLS = 512


def _resident(shape):
    nd = len(shape)
    return pl.BlockSpec(shape, lambda *_: (0,) * nd, pipeline_mode=pl.Buffered(1))


def _params(semantics):
    return pltpu.CompilerParams(dimension_semantics=semantics, vmem_limit_bytes=VMEM_LIMIT)


def _rms(x, gain):
    return x * lax.rsqrt(jnp.mean(x * x, axis=-1, keepdims=True) + EPS) * gain


def _sigmoid(x):
    return 1.0 / (1.0 + jnp.exp(-x))


def _softplus(x):
    return jnp.maximum(x, 0.0) + jnp.log1p(jnp.exp(-jnp.abs(x)))


def _dot(a, b):
    return jnp.dot(a, b, preferred_element_type=F32)


def _dot_nt(a, b):
    return lax.dot_general(a, b, (((1,), (1,)), ((), ())), preferred_element_type=F32)


def _dot_tn(a, b):
    return lax.dot_general(a, b, (((0,), (0,)), ((), ())), preferred_element_type=F32)


def _proj_cols(xn, w_ref, off, width, o_ref, scale=None):
    for c0 in range(0, width, MATMUL_COLS):
        cw = min(MATMUL_COLS, width - c0)
        acc = _dot(xn, w_ref[:, off + c0:off + c0 + cw])
        if scale is not None:
            acc = acc * scale
        o_ref[:, c0:c0 + cw] = acc.astype(o_ref.dtype)


_EVEN_MAIN = (SSD_INNER, SSD_XBC, GLA_KEY, GLA_KEY, GLA_VAL, GLA_VAL)


def _even_in_kernel(x_ref, g_ref, w_ref, ws_ref, w2_ref, gb_ref,
                    z_ref, xbc_ref, q_ref, k_ref, v_ref, r_ref, small_ref, logg_ref):
    xn = _rms(x_ref[...], g_ref[...]).astype(BF16)
    off = 0
    for width, o_ref, scale in zip(_EVEN_MAIN, (z_ref, xbc_ref, q_ref, k_ref, v_ref, r_ref),
                                   (None, None, GLA_HK ** -0.5, None, None, None)):
        _proj_cols(xn, w_ref, off, width, o_ref, scale)
        off += width
    small = _dot(xn, ws_ref[...])
    small_ref[...] = small
    logits = _dot(small.astype(BF16), w2_ref[...]) + gb_ref[...]
    logg_ref[...] = -_softplus(-logits) * (1.0 / GLA_TAU)


def _even_in_proj(x, gain, w_main, w_small, w2_pad, gla_b):
    t = x.shape[0]
    tm = min(TOKEN_TILE, t)
    row = lambda w: pl.BlockSpec((tm, w), lambda i: (i, 0))
    out_shape = [jax.ShapeDtypeStruct((t, w), BF16) for w in _EVEN_MAIN]
    out_shape += [jax.ShapeDtypeStruct((t, LANES), F32), jax.ShapeDtypeStruct((t, GLA_KEY), F32)]
    return pl.pallas_call(
        _even_in_kernel,
        grid=(t // tm,),
        in_specs=[row(D_MODEL), _resident((1, D_MODEL)), _resident(w_main.shape), _resident(w_small.shape),
                  _resident(w2_pad.shape), _resident((1, GLA_KEY))],
        out_specs=[row(w) for w in _EVEN_MAIN] + [row(LANES), row(GLA_KEY)],
        out_shape=out_shape,
        compiler_params=_params(("parallel",)),
        name="even_in_proj",
    )(x, gain, w_main, w_small, w2_pad, gla_b)


def _ssd_kernel(xbc_ref, z_ref, small_ref, cw_ref, cb_ref, dtb_ref, alog_ref, dskip_ref, nw_ref, expand_ref,
                tri_ref, y_ref, conv_buf, state_ref, y_buf):
    q = SSD_CHUNK

    @pl.when(pl.program_id(1) == 0)
    def _():
        conv_buf[0:SUBLANES, :] = jnp.zeros((SUBLANES, SSD_XBC), F32)
        state_ref[...] = jnp.zeros_like(state_ref)

    conv_buf[SUBLANES:SUBLANES + q, :] = xbc_ref[...].astype(F32)
    acc = cb_ref[...]
    for tap in range(SSD_CONV):
        start = SUBLANES - (SSD_CONV - 1) + tap
        acc = acc + cw_ref[tap:tap + 1, :] * conv_buf[start:start + q, :]
    conv_buf[0:SUBLANES, :] = conv_buf[q:q + SUBLANES, :]
    u = acc * _sigmoid(acc)
    xs = u[:, :SSD_INNER]
    bm = u[:, SSD_INNER:SSD_INNER + SSD_GROUPS * SSD_STATE]
    cm = u[:, SSD_INNER + SSD_GROUPS * SSD_STATE:]
    bm_t = bm.T

    lane = lax.broadcasted_iota(jnp.int32, (q, LANES), 1)
    dt = jnp.where(lane < SSD_HEADS, _softplus(small_ref[...] + dtb_ref[...]), 0.0)
    loga = dt * (-jnp.exp(alog_ref[...]))
    cum = jnp.dot(tri_ref[...], loga, precision=HIGHEST, preferred_element_type=F32)
    cum_row = cum.T
    expand = expand_ref[...]
    dt_full = jnp.dot(dt, expand, precision=HIGHEST, preferred_element_type=F32)
    cum_full = jnp.dot(cum, expand, precision=HIGHEST, preferred_element_type=F32)
    last_full = cum_full[q - 1:q, :]
    xdt = xs * dt_full
    xdt_b = xdt.astype(BF16)
    tril = lax.broadcasted_iota(jnp.int32, (q, q), 0) >= lax.broadcasted_iota(jnp.int32, (q, q), 1)

    for g in range(SSD_GROUPS):
        gs = slice(g * SSD_GROUP_WIDTH, (g + 1) * SSD_GROUP_WIDTH)
        cg = cm[:, g * SSD_STATE:(g + 1) * SSD_STATE].astype(BF16)
        bg = bm[:, g * SSD_STATE:(g + 1) * SSD_STATE].astype(BF16)
        scores = _dot_nt(cg, bg)
        st = state_ref[g]
        y_inter = _dot(cg, st.astype(BF16)) * jnp.exp(cum_full[:, gs])
        for e in range(SSD_GROUP_HEADS):
            h = g * SSD_GROUP_HEADS + e
            hs = slice(h * SSD_HEAD_DIM, (h + 1) * SSD_HEAD_DIM)
            seg = cum[:, h:h + 1] - cum_row[h:h + 1, :]
            decay = jnp.exp(jnp.where(tril, seg, NEG_BIG))
            ye = _dot((scores * decay).astype(BF16), xdt_b[:, hs])
            y_buf[:, hs] = ye + y_inter[:, e * SSD_HEAD_DIM:(e + 1) * SSD_HEAD_DIM]
        wts = jnp.exp(last_full[:, gs] - cum_full[:, gs])
        xw = (xdt[:, gs] * wts).astype(BF16)
        bg_t = bm_t[g * SSD_STATE:(g + 1) * SSD_STATE, :].astype(BF16)
        state_ref[g] = st * jnp.exp(last_full[:, gs]) + _dot(bg_t, xw)

    zf = z_ref[...].astype(F32)
    y = (y_buf[...] + dskip_ref[...] * xs) * (zf * _sigmoid(zf))
    for g in range(SSD_GROUPS):
        gs = slice(g * SSD_GROUP_WIDTH, (g + 1) * SSD_GROUP_WIDTH)
        y_ref[:, gs] = _rms(y[:, gs], nw_ref[:, gs]).astype(y_ref.dtype)


def _ssd_scan(xbc, z, small, conv_w, conv_b, dt_bias, a_log, d_skip, norm_w, batch):
    t = xbc.shape[0]
    q = SSD_CHUNK
    nc = t // batch // q
    pad = lambda v: jnp.pad(v.astype(F32), (0, LANES - v.shape[0]))[None, :]
    expand = (np.arange(LANES)[:, None] == (np.arange(SSD_INNER)[None, :] // SSD_HEAD_DIM)).astype(np.float32)
    tri = np.tril(np.ones((q, q), np.float32))
    row = lambda w: pl.BlockSpec((q, w), lambda b, c: (b * nc + c, 0))
    return pl.pallas_call(
        _ssd_kernel,
        grid=(batch, nc),
        in_specs=[row(SSD_XBC), row(SSD_INNER), row(LANES), _resident((SSD_CONV, SSD_XBC)), _resident((1, SSD_XBC)),
                  _resident((1, LANES)), _resident((1, LANES)), _resident((1, SSD_INNER)), _resident((1, SSD_INNER)),
                  _resident((LANES, SSD_INNER)), _resident((q, q))],
        out_specs=row(SSD_INNER),
        out_shape=jax.ShapeDtypeStruct((t, SSD_INNER), BF16),
        scratch_shapes=[pltpu.VMEM((q + SUBLANES, SSD_XBC), F32),
                        pltpu.VMEM((SSD_GROUPS, SSD_STATE, SSD_GROUP_WIDTH), F32),
                        pltpu.VMEM((q, SSD_INNER), F32)],
        compiler_params=_params(("arbitrary", "arbitrary")),
        name="ssd_scan",
    )(xbc, z, small, conv_w, conv_b[None, :], pad(dt_bias), pad(a_log),
      jnp.repeat(d_skip.astype(F32), SSD_HEAD_DIM)[None, :], norm_w[None, :], jnp.asarray(expand), jnp.asarray(tri))


def _gla_kernel(q_ref, k_ref, v_ref, g_ref, r_ref, nw_ref, tri_ref, o_ref,
                state_ref, k_pad, c_pad, v_pad, term_buf):
    n = GLA_CHUNK
    sub = GLA_SUB

    @pl.when(pl.program_id(1) == 0)
    def _():
        state_ref[...] = jnp.zeros_like(state_ref)

    k_pad[0:sub, :] = jnp.zeros((sub, GLA_HK), F32)
    c_pad[0:sub, :] = jnp.zeros((sub, GLA_HK), F32)
    v_pad[0:sub, :] = jnp.zeros((sub, GLA_HV), F32)
    ones = jnp.ones((GLA_HK, GLA_HV), BF16)
    row = lax.broadcasted_iota(jnp.int32, (n, n), 0)
    col = lax.broadcasted_iota(jnp.int32, (n, n), 1)
    far = (row - col) >= sub

    for h in range(GLA_HEADS):
        ks = slice(h * GLA_HK, (h + 1) * GLA_HK)
        vs = slice(h * GLA_HV, (h + 1) * GLA_HV)
        qh = q_ref[:, ks].astype(F32)
        kh = k_ref[:, ks].astype(F32)
        vh_b = v_ref[:, vs]
        cum = jnp.dot(tri_ref[...], g_ref[:, ks], precision=HIGHEST, preferred_element_type=F32)
        last = cum[n - 1:n, :]
        st = state_ref[h]

        o = _dot_nt((qh * jnp.exp(cum)).astype(BF16), st.astype(BF16))

        blocks = [jnp.zeros((sub, n), F32)]
        for i in range(1, n // sub):
            ref = cum[i * sub - 1:i * sub, :]
            qt = qh[i * sub:(i + 1) * sub, :] * jnp.exp(cum[i * sub:(i + 1) * sub, :] - ref)
            kt = kh * jnp.exp(jnp.minimum(ref - cum, 0.0))
            blocks.append(_dot_nt(qt.astype(BF16), kt.astype(BF16)))
        a_far = jnp.where(far, jnp.concatenate(blocks, axis=0), 0.0)
        o = o + _dot(a_far.astype(BF16), vh_b)

        k_pad[sub:sub + n, :] = kh
        c_pad[sub:sub + n, :] = cum
        v_pad[sub:sub + n, :] = vh_b.astype(F32)
        for d in range(sub):
            kd = k_pad[sub - d:sub - d + n, :]
            cd = c_pad[sub - d:sub - d + n, :]
            term_buf[d * n:(d + 1) * n, :] = (qh * kd * jnp.exp(cum - cd)).astype(BF16)
        near = _dot(term_buf[...], ones)
        for d in range(sub):
            o = o + near[d * n:(d + 1) * n, :] * v_pad[sub - d:sub - d + n, :]

        state_ref[h] = st * jnp.exp(last) + _dot_tn(vh_b, (kh * jnp.exp(last - cum)).astype(BF16))

        rf = r_ref[:, vs].astype(F32)
        o_ref[:, vs] = (_rms(o, nw_ref[...]) * (rf * _sigmoid(rf))).astype(o_ref.dtype)


def _gla_scan(q, k, v, logg, r, norm_w, batch):
    t = q.shape[0]
    n = GLA_CHUNK
    nc = t // batch // n
    tri = np.tril(np.ones((n, n), np.float32))
    row = lambda w: pl.BlockSpec((n, w), lambda b, c: (b * nc + c, 0))
    return pl.pallas_call(
        _gla_kernel,
        grid=(batch, nc),
        in_specs=[row(GLA_KEY), row(GLA_KEY), row(GLA_VAL), row(GLA_KEY), row(GLA_VAL),
                  _resident((1, GLA_HV)), _resident((n, n))],
        out_specs=row(GLA_VAL),
        out_shape=jax.ShapeDtypeStruct((t, GLA_VAL), BF16),
        scratch_shapes=[pltpu.VMEM((GLA_HEADS, GLA_HV, GLA_HK), F32),
                        pltpu.VMEM((n + GLA_SUB, GLA_HK), F32),
                        pltpu.VMEM((n + GLA_SUB, GLA_HK), F32),
                        pltpu.VMEM((n + GLA_SUB, GLA_HV), F32),
                        pltpu.VMEM((GLA_SUB * n, GLA_HK), BF16)],
        compiler_params=_params(("arbitrary", "arbitrary")),
        name="gla_scan",
    )(q, k, v, logg, r, norm_w[None, :], jnp.asarray(tri))


def _proj_residual_kernel(*refs, widths):
    x_ref = refs[0]
    a_refs = refs[1:1 + len(widths)]
    w_ref = refs[1 + len(widths)]
    o_ref = refs[2 + len(widths)]
    for c0 in range(0, D_MODEL, MATMUL_COLS):
        cs = slice(c0, c0 + MATMUL_COLS)
        acc = x_ref[:, cs]
        off = 0
        for a_ref, width in zip(a_refs, widths):
            acc = acc + _dot(a_ref[...], w_ref[off:off + width, cs])
            off += width
        o_ref[:, cs] = acc


def _proj_residual(x, acts, w):
    t = x.shape[0]
    tm = min(TOKEN_TILE, t)
    widths = tuple(a.shape[1] for a in acts)
    row = lambda wd: pl.BlockSpec((tm, wd), lambda i: (i, 0))
    return pl.pallas_call(
        functools.partial(_proj_residual_kernel, widths=widths),
        grid=(t // tm,),
        in_specs=[row(D_MODEL)] + [row(wd) for wd in widths] + [_resident(w.shape)],
        out_specs=row(D_MODEL),
        out_shape=jax.ShapeDtypeStruct((t, D_MODEL), F32),
        compiler_params=_params(("parallel",)),
        name="proj_residual",
    )(x, *acts, w)


def _norm_proj_kernel(x_ref, g_ref, w_ref, *o_refs):
    xn = _rms(x_ref[...], g_ref[...]).astype(BF16)
    off = 0
    for o_ref in o_refs:
        width = o_ref.shape[1]
        _proj_cols(xn, w_ref, off, width, o_ref)
        off += width


def _norm_proj(x, gain, w, widths):
    t = x.shape[0]
    tm = min(TOKEN_TILE, t)
    row = lambda wd: pl.BlockSpec((tm, wd), lambda i: (i, 0))
    return pl.pallas_call(
        _norm_proj_kernel,
        grid=(t // tm,),
        in_specs=[row(D_MODEL), _resident((1, D_MODEL)), _resident(w.shape)],
        out_specs=[row(wd) for wd in widths],
        out_shape=[jax.ShapeDtypeStruct((t, wd), BF16) for wd in widths],
        compiler_params=_params(("parallel",)),
        name="norm_proj",
    )(x, gain, w)


def _diff_attn_kernel(slope_ref, q_ref, k_ref, v_ref, bias_ref, lam_ref, nw_ref, o_ref,
                      acc_ref, m_ref, l_ref, *, lambda_init):
    tile = q_ref.shape[0]
    h = pl.program_id(1)
    i = pl.program_id(2)
    slope = slope_ref[h]

    q = q_ref[...]
    lane = lax.broadcasted_iota(jnp.int32, q.shape, 1)
    zero = jnp.zeros_like(q)
    q_halves = (jnp.where(lane < DIFF_HEAD_DIM, q * (DIFF_HEAD_DIM ** -0.5), zero),
                jnp.where(lane >= DIFF_HEAD_DIM, q * (DIFF_HEAD_DIM ** -0.5), zero))

    acc_ref[...] = jnp.zeros_like(acc_ref)
    l_ref[...] = jnp.zeros_like(l_ref)
    m_ref[...] = jnp.full_like(m_ref, NEG_BIG)

    def kv_step(j, masked):
        start = pl.multiple_of(j * tile, tile)
        kj = k_ref[pl.ds(start, tile), :]
        vj = v_ref[pl.ds(start, tile), :]
        off = -slope * (tile * (i - j)).astype(F32)
        for m in range(2):
            s = _dot_nt(q_halves[m], kj) + bias_ref[...]
            if masked:
                s = jnp.where(bias_ref[...] <= 0.0, s, NEG_BIG)
            m_old = m_ref[m]
            m_new = jnp.maximum(m_old, jnp.max(s, axis=-1, keepdims=True) + off)
            alpha = jnp.exp(m_old - m_new)
            p = jnp.exp(s + (off - m_new))
            l_ref[m] = alpha * l_ref[m] + jnp.sum(p, axis=-1, keepdims=True)
            acc_ref[m] = alpha * acc_ref[m] + _dot(p.astype(BF16), vj)
            m_ref[m] = m_new

    def body(j, carry):
        kv_step(j, False)
        return carry

    lax.fori_loop(0, i, body, 0)
    kv_step(i, True)

    lam = (jnp.exp(jnp.sum(lam_ref[0:1, :] * lam_ref[1:2, :], axis=-1, keepdims=True))
           - jnp.exp(jnp.sum(lam_ref[2:3, :] * lam_ref[3:4, :], axis=-1, keepdims=True)) + lambda_init)
    o = acc_ref[0] / l_ref[0] - lam * (acc_ref[1] / l_ref[1])
    o_ref[...] = (_rms(o, nw_ref[...]) * (1.0 - lambda_init)).astype(o_ref.dtype)


def _alibi_slopes(n):
    start = 2.0 ** (-8.0 / n)
    return np.array([start ** (i + 1) for i in range(n)], dtype=np.float32)


def _diff_attention(q, k, v, lam_vecs, subln, lambda_init, batch):
    t = q.shape[0]
    seq = t // batch
    tile = min(ATTN_TILE, seq)
    nq = seq // tile
    slopes = _alibi_slopes(DIFF_HEADS)
    rel = (np.arange(tile)[None, :] - np.arange(tile)[:, None]).astype(np.float32)
    bias = slopes[:, None, None] * rel[None]
    head_cols = lambda b, h, i: (b * nq + i, h)
    return pl.pallas_call(
        functools.partial(_diff_attn_kernel, lambda_init=lambda_init),
        grid=(batch, DIFF_HEADS, nq),
        in_specs=[pl.BlockSpec(memory_space=pltpu.SMEM),
                  pl.BlockSpec((tile, LANES), head_cols),
                  pl.BlockSpec((seq, LANES), lambda b, h, i: (b, h)),
                  pl.BlockSpec((seq, LANES), lambda b, h, i: (b, h)),
                  pl.BlockSpec((None, tile, tile), lambda b, h, i: (h, 0, 0)),
                  _resident((4, DIFF_HEAD_DIM)), _resident((1, DIFF_V_DIM))],
        out_specs=pl.BlockSpec((tile, LANES), head_cols),
        out_shape=jax.ShapeDtypeStruct((t, DIFF_HEADS * DIFF_V_DIM), BF16),
        scratch_shapes=[pltpu.VMEM((2, tile, DIFF_V_DIM), F32), pltpu.VMEM((2, tile, 1), F32),
                        pltpu.VMEM((2, tile, 1), F32)],
        compiler_params=_params(("arbitrary", "arbitrary", "arbitrary")),
        name="diff_attention",
    )(jnp.asarray(slopes), q, k, v, jnp.asarray(bias), lam_vecs, subln[None, :])


def _cross_attn_kernel(x_ref, g_ref, wq_ref, k_ref, v_ref, wo_ref, o_ref, att_buf):
    x = x_ref[...]
    xn = _rms(x, g_ref[...]).astype(BF16)
    for h in range(X_HEADS):
        hs = slice(h * X_HEAD_DIM, (h + 1) * X_HEAD_DIM)
        qh = _dot(xn, wq_ref[:, hs]).astype(BF16)
        s = _dot_nt(qh, k_ref[:, hs]) * (X_HEAD_DIM ** -0.5)
        p = jnp.exp(s - jnp.max(s, axis=-1, keepdims=True))
        p = p / jnp.sum(p, axis=-1, keepdims=True)
        att_buf[:, hs] = _dot(p.astype(BF16), v_ref[:, hs]).astype(BF16)
    for c0 in range(0, D_MODEL, MATMUL_COLS):
        cs = slice(c0, c0 + MATMUL_COLS)
        o_ref[:, cs] = x[:, cs] + _dot(att_buf[...], wo_ref[:, cs])


def _cross_attention(x, gain, wq, k_mem, v_mem, wo, batch):
    t = x.shape[0]
    seq = t // batch
    tm = min(TOKEN_TILE, seq)
    per = seq // tm
    n_mem = k_mem.shape[0] // batch
    row = pl.BlockSpec((tm, D_MODEL), lambda i: (i, 0))
    mem = pl.BlockSpec((n_mem, D_MODEL), lambda i: (i // per, 0))
    return pl.pallas_call(
        _cross_attn_kernel,
        grid=(t // tm,),
        in_specs=[row, _resident((1, D_MODEL)), _resident(wq.shape), mem, mem, _resident(wo.shape)],
        out_specs=row,
        out_shape=jax.ShapeDtypeStruct((t, D_MODEL), F32),
        scratch_shapes=[pltpu.VMEM((tm, D_MODEL), BF16)],
        compiler_params=_params(("parallel",)),
        name="cross_attention",
    )(x, gain, wq, k_mem, v_mem, wo)


def _mlp_kernel(x_ref, g_ref, w1_ref, w2_ref, fg_ref, o_ref, acc_ref, *, final_norm):
    x = x_ref[...]
    xn = _rms(x, g_ref[...]).astype(BF16)
    acc_ref[...] = x
    for c0 in range(0, D_FF, MATMUL_COLS):
        hid = jnp.maximum(_dot(xn, w1_ref[:, c0:c0 + MATMUL_COLS]), 0.0)
        acc_ref[...] += _dot((hid * hid).astype(BF16), w2_ref[c0:c0 + MATMUL_COLS, :])
    out = acc_ref[...]
    if final_norm:
        out = _rms(out, fg_ref[...])
    o_ref[...] = out


def _mlp(x, gain, w1, w2, final_gain, final_norm):
    t = x.shape[0]
    tm = min(TOKEN_TILE, t)
    row = pl.BlockSpec((tm, D_MODEL), lambda i: (i, 0))
    return pl.pallas_call(
        functools.partial(_mlp_kernel, final_norm=final_norm),
        grid=(t // tm,),
        in_specs=[row, _resident((1, D_MODEL)), _resident(w1.shape), _resident(w2.shape), _resident((1, D_MODEL))],
        out_specs=row,
        out_shape=jax.ShapeDtypeStruct((t, D_MODEL), F32),
        scratch_shapes=[pltpu.VMEM((tm, D_MODEL), F32)],
        compiler_params=_params(("parallel",)),
        name="mlp",
    )(x, gain, w1, w2, final_gain)


def _even_weights(w_in, gla_w2):
    z_end = SSD_INNER
    xbc_end = z_end + SSD_XBC
    dt_end = xbc_end + SSD_HEADS
    v_end = dt_end + 2 * GLA_KEY + GLA_VAL
    glr_end = v_end + GLA_RANK
    w_main = jnp.concatenate([w_in[:, :xbc_end], w_in[:, dt_end:v_end], w_in[:, glr_end:]], axis=1).astype(BF16)
    w_small = jnp.concatenate([w_in[:, xbc_end:dt_end], w_in[:, v_end:glr_end],
                               jnp.zeros((D_MODEL, LANES - SSD_HEADS - GLA_RANK), w_in.dtype)], axis=1).astype(BF16)
    w2_pad = jnp.zeros((LANES, GLA_KEY), F32).at[SSD_HEADS:SSD_HEADS + GLA_RANK].set(gla_w2).astype(BF16)
    return w_main, w_small, w2_pad


def kernel(x, mem, ev_norm, ev_w_in, ev_conv_w, ev_conv_b, ev_dt_bias, ev_a_log, ev_d_skip, ev_ssd_norm, ev_gla_w2, ev_gla_b, ev_gla_norm, ev_w_out, od_norm, od_w_qkv, od_lam_q1, od_lam_k1, od_lam_q2, od_lam_k2, od_subln, od_w_o, xa_norm, xa_mem_norm, xa_wq, xa_wkv, xa_wo, mlp_norm, mlp_w1, mlp_w2, final_norm):
    batch, seq, d = x.shape
    n_mem = mem.shape[1]
    n_layers = xa_norm.shape[0]
    xf = x.reshape(batch * seq, d)
    memf = mem.reshape(batch * n_mem, d)
    for layer in range(n_layers):
        i = layer // 2
        if layer % 2 == 0:
            w_main, w_small, w2_pad = _even_weights(ev_w_in[i], ev_gla_w2[i])
            z, xbc, q, k, v, r, small, logg = _even_in_proj(xf, ev_norm[i][None, :], w_main, w_small, w2_pad,
                                                            ev_gla_b[i][None, :])
            y = _ssd_scan(xbc, z, small, ev_conv_w[i], ev_conv_b[i], ev_dt_bias[i], ev_a_log[i], ev_d_skip[i],
                          ev_ssd_norm[i], batch)
            o = _gla_scan(q, k, v, logg, r, ev_gla_norm[i], batch)
            xf = _proj_residual(xf, [y, o], ev_w_out[i].astype(BF16))
        else:
            lambda_init = 0.8 - 0.6 * math.exp(-0.3 * layer)
            q, k, v = _norm_proj(xf, od_norm[i][None, :], od_w_qkv[i].astype(BF16),
                                 (DIFF_QK, DIFF_QK, DIFF_HEADS * DIFF_V_DIM))
            lam_vecs = jnp.stack([od_lam_q1[i], od_lam_k1[i], od_lam_q2[i], od_lam_k2[i]]).astype(F32)
            att = _diff_attention(q, k, v, lam_vecs, od_subln[i], lambda_init, batch)
            xf = _proj_residual(xf, [att], od_w_o[i].astype(BF16))
        k_mem, v_mem = _norm_proj(memf, xa_mem_norm[layer][None, :], xa_wkv[layer].astype(BF16), (d, d))
        xf = _cross_attention(xf, xa_norm[layer][None, :], xa_wq[layer].astype(BF16), k_mem, v_mem,
                              xa_wo[layer].astype(BF16), batch)
        xf = _mlp(xf, mlp_norm[layer][None, :], mlp_w1[layer].astype(BF16), mlp_w2[layer].astype(BF16),
                  final_norm[None, :], layer == n_layers - 1)
    return xf.reshape(batch, seq, d)
```

```python
import functools
import math

import numpy as np
import jax
import jax.numpy as jnp
from jax import lax
from jax.experimental import pallas as pl
from jax.experimental.pallas import tpu as pltpu

F32 = jnp.float32
BF16 = jnp.bfloat16
HIGHEST = lax.Precision.HIGHEST
EPS = 1e-5
NEG_BIG = -1e30

D_MODEL = 1024
N_LAYERS = 4

SSD_HEADS = 16
SSD_HEAD_DIM = 64
SSD_INNER = SSD_HEADS * SSD_HEAD_DIM
SSD_GROUPS = 2
SSD_STATE = 64
SSD_CONV = 4
SSD_CHUNK = 128
SSD_XBC = SSD_INNER + 2 * SSD_GROUPS * SSD_STATE
SSD_GROUP_WIDTH = SSD_INNER // SSD_GROUPS
SSD_GROUP_HEADS = SSD_HEADS // SSD_GROUPS

GLA_HEADS = 4
GLA_KEY = 512
GLA_VAL = 1024
GLA_HK = GLA_KEY // GLA_HEADS
GLA_HV = GLA_VAL // GLA_HEADS
GLA_RANK = 16
GLA_TAU = 16.0
GLA_CHUNK = 64
GLA_SUB = 16

DIFF_HEADS = 8
DIFF_HEAD_DIM = 64
DIFF_V_DIM = 2 * DIFF_HEAD_DIM
DIFF_QK = DIFF_HEADS * 2 * DIFF_HEAD_DIM
LOG2E = 1.4426950408889634
DIFF_Q_SCALE = DIFF_HEAD_DIM ** -0.5 * LOG2E
ALIBI_PIECES = 3

X_HEADS = 4
X_HEAD_DIM = D_MODEL // X_HEADS
D_FF = 4 * D_MODEL

LANES = 128
SUBLANES = 8
VMEM_LIMIT = 56 * 1024 * 1024
TOKEN_TILE = 512
ATTN_TILE = 1024
MATMUL_COLS = 512


def _resident(shape):
    nd = len(shape)
    return pl.BlockSpec(shape, lambda *_: (0,) * nd, pipeline_mode=pl.Buffered(1))


def _params(semantics):
    return pltpu.CompilerParams(dimension_semantics=semantics, vmem_limit_bytes=VMEM_LIMIT)


def _rms(x, gain):
    return x * lax.rsqrt(jnp.mean(x * x, axis=-1, keepdims=True) + EPS) * gain


def _sigmoid(x):
    return 1.0 / (1.0 + jnp.exp(-x))


def _softplus(x):
    return jnp.maximum(x, 0.0) + jnp.log1p(jnp.exp(-jnp.abs(x)))


def _dot(a, b):
    return jnp.dot(a, b, preferred_element_type=F32)


def _dot_nt(a, b):
    return lax.dot_general(a, b, (((1,), (1,)), ((), ())), preferred_element_type=F32)


def _dot_tn(a, b):
    return lax.dot_general(a, b, (((0,), (0,)), ((), ())), preferred_element_type=F32)


def _proj_cols(xn, w_ref, off, width, o_ref, scale=None):
    for c0 in range(0, width, MATMUL_COLS):
        cw = min(MATMUL_COLS, width - c0)
        acc = _dot(xn, w_ref[:, off + c0:off + c0 + cw])
        if scale is not None:
            acc = acc * scale
        o_ref[:, c0:c0 + cw] = acc.astype(o_ref.dtype)


_EVEN_MAIN = (SSD_INNER, SSD_XBC, GLA_KEY, GLA_KEY, GLA_VAL, GLA_VAL)


def _even_in_kernel(x_ref, g_ref, w_ref, ws_ref, w2_ref, gb_ref,
                    z_ref, xbc_ref, q_ref, k_ref, v_ref, r_ref, small_ref, logg_ref):
    xn = _rms(x_ref[...], g_ref[...]).astype(BF16)
    off = 0
    for width, o_ref, scale in zip(_EVEN_MAIN, (z_ref, xbc_ref, q_ref, k_ref, v_ref, r_ref),
                                   (None, None, GLA_HK ** -0.5, None, None, None)):
        _proj_cols(xn, w_ref, off, width, o_ref, scale)
        off += width
    small = _dot(xn, ws_ref[...])
    small_ref[...] = small
    logits = _dot(small.astype(BF16), w2_ref[...]) + gb_ref[...]
    logg_ref[...] = -_softplus(-logits) * (1.0 / GLA_TAU)


def _even_in_proj(x, gain, w_main, w_small, w2_pad, gla_b):
    t = x.shape[0]
    tm = min(TOKEN_TILE, t)
    row = lambda w: pl.BlockSpec((tm, w), lambda i: (i, 0))
    out_shape = [jax.ShapeDtypeStruct((t, w), BF16) for w in _EVEN_MAIN]
    out_shape += [jax.ShapeDtypeStruct((t, LANES), F32), jax.ShapeDtypeStruct((t, GLA_KEY), F32)]
    return pl.pallas_call(
        _even_in_kernel,
        grid=(t // tm,),
        in_specs=[row(D_MODEL), _resident((1, D_MODEL)), _resident(w_main.shape), _resident(w_small.shape),
                  _resident(w2_pad.shape), _resident((1, GLA_KEY))],
        out_specs=[row(w) for w in _EVEN_MAIN] + [row(LANES), row(GLA_KEY)],
        out_shape=out_shape,
        compiler_params=_params(("parallel",)),
        name="even_in_proj",
    )(x, gain, w_main, w_small, w2_pad, gla_b)


def _ssd_kernel(xbc_ref, z_ref, small_ref, cw_ref, cb_ref, dtb_ref, alog_ref, dskip_ref, nw_ref, expand_ref,
                tri_ref, y_ref, conv_buf, state_ref, y_buf):
    q = SSD_CHUNK

    @pl.when(pl.program_id(1) == 0)
    def _():
        conv_buf[0:SUBLANES, :] = jnp.zeros((SUBLANES, SSD_XBC), F32)
        state_ref[...] = jnp.zeros_like(state_ref)

    conv_buf[SUBLANES:SUBLANES + q, :] = xbc_ref[...].astype(F32)
    acc = cb_ref[...]
    for tap in range(SSD_CONV):
        start = SUBLANES - (SSD_CONV - 1) + tap
        acc = acc + cw_ref[tap:tap + 1, :] * conv_buf[start:start + q, :]
    conv_buf[0:SUBLANES, :] = conv_buf[q:q + SUBLANES, :]
    u = acc * _sigmoid(acc)
    xs = u[:, :SSD_INNER]
    bm = u[:, SSD_INNER:SSD_INNER + SSD_GROUPS * SSD_STATE]
    cm = u[:, SSD_INNER + SSD_GROUPS * SSD_STATE:]
    bm_t = bm.T

    lane = lax.broadcasted_iota(jnp.int32, (q, LANES), 1)
    dt = jnp.where(lane < SSD_HEADS, _softplus(small_ref[...] + dtb_ref[...]), 0.0)
    loga = dt * (-jnp.exp(alog_ref[...]))
    cum = jnp.dot(tri_ref[...], loga, precision=HIGHEST, preferred_element_type=F32)
    cum_row = cum.T
    expand = expand_ref[...]
    dt_full = jnp.dot(dt, expand, precision=HIGHEST, preferred_element_type=F32)
    cum_full = jnp.dot(cum, expand, precision=HIGHEST, preferred_element_type=F32)
    last_full = cum_full[q - 1:q, :]
    xdt = xs * dt_full
    xdt_b = xdt.astype(BF16)
    tril = lax.broadcasted_iota(jnp.int32, (q, q), 0) >= lax.broadcasted_iota(jnp.int32, (q, q), 1)

    for g in range(SSD_GROUPS):
        gs = slice(g * SSD_GROUP_WIDTH, (g + 1) * SSD_GROUP_WIDTH)
        cg = cm[:, g * SSD_STATE:(g + 1) * SSD_STATE].astype(BF16)
        bg = bm[:, g * SSD_STATE:(g + 1) * SSD_STATE].astype(BF16)
        scores = _dot_nt(cg, bg)
        st = state_ref[g]
        y_inter = _dot(cg, st.astype(BF16)) * jnp.exp(cum_full[:, gs])
        for e in range(SSD_GROUP_HEADS):
            h = g * SSD_GROUP_HEADS + e
            hs = slice(h * SSD_HEAD_DIM, (h + 1) * SSD_HEAD_DIM)
            seg = cum[:, h:h + 1] - cum_row[h:h + 1, :]
            decay = jnp.exp(jnp.where(tril, seg, NEG_BIG))
            ye = _dot((scores * decay).astype(BF16), xdt_b[:, hs])
            y_buf[:, hs] = ye + y_inter[:, e * SSD_HEAD_DIM:(e + 1) * SSD_HEAD_DIM]
        wts = jnp.exp(last_full[:, gs] - cum_full[:, gs])
        xw = (xdt[:, gs] * wts).astype(BF16)
        bg_t = bm_t[g * SSD_STATE:(g + 1) * SSD_STATE, :].astype(BF16)
        state_ref[g] = st * jnp.exp(last_full[:, gs]) + _dot(bg_t, xw)

    zf = z_ref[...].astype(F32)
    y = (y_buf[...] + dskip_ref[...] * xs) * (zf * _sigmoid(zf))
    for g in range(SSD_GROUPS):
        gs = slice(g * SSD_GROUP_WIDTH, (g + 1) * SSD_GROUP_WIDTH)
        y_ref[:, gs] = _rms(y[:, gs], nw_ref[:, gs]).astype(y_ref.dtype)


def _ssd_scan(xbc, z, small, conv_w, conv_b, dt_bias, a_log, d_skip, norm_w, batch):
    t = xbc.shape[0]
    q = SSD_CHUNK
    nc = t // batch // q
    pad = lambda v: jnp.pad(v.astype(F32), (0, LANES - v.shape[0]))[None, :]
    expand = (np.arange(LANES)[:, None] == (np.arange(SSD_INNER)[None, :] // SSD_HEAD_DIM)).astype(np.float32)
    tri = np.tril(np.ones((q, q), np.float32))
    row = lambda w: pl.BlockSpec((q, w), lambda b, c: (b * nc + c, 0))
    return pl.pallas_call(
        _ssd_kernel,
        grid=(batch, nc),
        in_specs=[row(SSD_XBC), row(SSD_INNER), row(LANES), _resident((SSD_CONV, SSD_XBC)), _resident((1, SSD_XBC)),
                  _resident((1, LANES)), _resident((1, LANES)), _resident((1, SSD_INNER)), _resident((1, SSD_INNER)),
                  _resident((LANES, SSD_INNER)), _resident((q, q))],
        out_specs=row(SSD_INNER),
        out_shape=jax.ShapeDtypeStruct((t, SSD_INNER), BF16),
        scratch_shapes=[pltpu.VMEM((q + SUBLANES, SSD_XBC), F32),
                        pltpu.VMEM((SSD_GROUPS, SSD_STATE, SSD_GROUP_WIDTH), F32),
                        pltpu.VMEM((q, SSD_INNER), F32)],
        compiler_params=_params(("arbitrary", "arbitrary")),
        name="ssd_scan",
    )(xbc, z, small, conv_w, conv_b[None, :], pad(dt_bias), pad(a_log),
      jnp.repeat(d_skip.astype(F32), SSD_HEAD_DIM)[None, :], norm_w[None, :], jnp.asarray(expand), jnp.asarray(tri))


def _gla_kernel(q_ref, k_ref, v_ref, g_ref, r_ref, nw_ref, tri_ref, o_ref,
                state_ref, k_pad, c_pad, v_pad, term_buf):
    n = GLA_CHUNK
    sub = GLA_SUB

    @pl.when(pl.program_id(1) == 0)
    def _():
        state_ref[...] = jnp.zeros_like(state_ref)

    k_pad[0:sub, :] = jnp.zeros((sub, GLA_HK), F32)
    c_pad[0:sub, :] = jnp.zeros((sub, GLA_HK), F32)
    v_pad[0:sub, :] = jnp.zeros((sub, GLA_HV), F32)
    ones = jnp.ones((GLA_HK, GLA_HV), BF16)
    row = lax.broadcasted_iota(jnp.int32, (n, n), 0)
    col = lax.broadcasted_iota(jnp.int32, (n, n), 1)
    far = (row - col) >= sub

    for h in range(GLA_HEADS):
        ks = slice(h * GLA_HK, (h + 1) * GLA_HK)
        vs = slice(h * GLA_HV, (h + 1) * GLA_HV)
        qh = q_ref[:, ks].astype(F32)
        kh = k_ref[:, ks].astype(F32)
        vh_b = v_ref[:, vs]
        cum = jnp.dot(tri_ref[...], g_ref[:, ks], precision=HIGHEST, preferred_element_type=F32)
        last = cum[n - 1:n, :]
        st = state_ref[h]

        o = _dot_nt((qh * jnp.exp(cum)).astype(BF16), st.astype(BF16))

        blocks = [jnp.zeros((sub, n), F32)]
        for i in range(1, n // sub):
            ref = cum[i * sub - 1:i * sub, :]
            qt = qh[i * sub:(i + 1) * sub, :] * jnp.exp(cum[i * sub:(i + 1) * sub, :] - ref)
            kt = kh * jnp.exp(jnp.minimum(ref - cum, 0.0))
            blocks.append(_dot_nt(qt.astype(BF16), kt.astype(BF16)))
        a_far = jnp.where(far, jnp.concatenate(blocks, axis=0), 0.0)
        o = o + _dot(a_far.astype(BF16), vh_b)

        k_pad[sub:sub + n, :] = kh
        c_pad[sub:sub + n, :] = cum
        v_pad[sub:sub + n, :] = vh_b.astype(F32)
        for d in range(sub):
            kd = k_pad[sub - d:sub - d + n, :]
            cd = c_pad[sub - d:sub - d + n, :]
            term_buf[d * n:(d + 1) * n, :] = (qh * kd * jnp.exp(cum - cd)).astype(BF16)
        near = _dot(term_buf[...], ones)
        for d in range(sub):
            o = o + near[d * n:(d + 1) * n, :] * v_pad[sub - d:sub - d + n, :]

        state_ref[h] = st * jnp.exp(last) + _dot_tn(vh_b, (kh * jnp.exp(last - cum)).astype(BF16))

        rf = r_ref[:, vs].astype(F32)
        o_ref[:, vs] = (_rms(o, nw_ref[...]) * (rf * _sigmoid(rf))).astype(o_ref.dtype)


def _gla_scan(q, k, v, logg, r, norm_w, batch):
    t = q.shape[0]
    n = GLA_CHUNK
    nc = t // batch // n
    tri = np.tril(np.ones((n, n), np.float32))
    row = lambda w: pl.BlockSpec((n, w), lambda b, c: (b * nc + c, 0))
    return pl.pallas_call(
        _gla_kernel,
        grid=(batch, nc),
        in_specs=[row(GLA_KEY), row(GLA_KEY), row(GLA_VAL), row(GLA_KEY), row(GLA_VAL),
                  _resident((1, GLA_HV)), _resident((n, n))],
        out_specs=row(GLA_VAL),
        out_shape=jax.ShapeDtypeStruct((t, GLA_VAL), BF16),
        scratch_shapes=[pltpu.VMEM((GLA_HEADS, GLA_HV, GLA_HK), F32),
                        pltpu.VMEM((n + GLA_SUB, GLA_HK), F32),
                        pltpu.VMEM((n + GLA_SUB, GLA_HK), F32),
                        pltpu.VMEM((n + GLA_SUB, GLA_HV), F32),
                        pltpu.VMEM((GLA_SUB * n, GLA_HK), BF16)],
        compiler_params=_params(("arbitrary", "arbitrary")),
        name="gla_scan",
    )(q, k, v, logg, r, norm_w[None, :], jnp.asarray(tri))


def _proj_residual_kernel(*refs, widths):
    x_ref = refs[0]
    a_refs = refs[1:1 + len(widths)]
    w_ref = refs[1 + len(widths)]
    o_ref = refs[2 + len(widths)]
    for c0 in range(0, D_MODEL, MATMUL_COLS):
        cs = slice(c0, c0 + MATMUL_COLS)
        acc = x_ref[:, cs]
        off = 0
        for a_ref, width in zip(a_refs, widths):
            acc = acc + _dot(a_ref[...], w_ref[off:off + width, cs])
            off += width
        o_ref[:, cs] = acc


def _proj_residual(x, acts, w):
    t = x.shape[0]
    tm = min(TOKEN_TILE, t)
    widths = tuple(a.shape[1] for a in acts)
    row = lambda wd: pl.BlockSpec((tm, wd), lambda i: (i, 0))
    return pl.pallas_call(
        functools.partial(_proj_residual_kernel, widths=widths),
        grid=(t // tm,),
        in_specs=[row(D_MODEL)] + [row(wd) for wd in widths] + [_resident(w.shape)],
        out_specs=row(D_MODEL),
        out_shape=jax.ShapeDtypeStruct((t, D_MODEL), F32),
        compiler_params=_params(("parallel",)),
        name="proj_residual",
    )(x, *acts, w)


def _norm_proj_kernel(x_ref, g_ref, w_ref, *o_refs):
    xn = _rms(x_ref[...], g_ref[...]).astype(BF16)
    off = 0
    for o_ref in o_refs:
        width = o_ref.shape[1]
        _proj_cols(xn, w_ref, off, width, o_ref)
        off += width


def _norm_proj(x, gain, w, widths):
    t = x.shape[0]
    tm = min(TOKEN_TILE, t)
    row = lambda wd: pl.BlockSpec((tm, wd), lambda i: (i, 0))
    return pl.pallas_call(
        _norm_proj_kernel,
        grid=(t // tm,),
        in_specs=[row(D_MODEL), _resident((1, D_MODEL)), _resident(w.shape)],
        out_specs=[row(wd) for wd in widths],
        out_shape=[jax.ShapeDtypeStruct((t, wd), BF16) for wd in widths],
        compiler_params=_params(("parallel",)),
        name="norm_proj",
    )(x, gain, w)


def _qkv_proj_kernel(x_ref, g_ref, wqt_ref, wk_ref, wvt_ref, qt_ref, k_ref, vt_ref):
    xn = _rms(x_ref[...], g_ref[...]).astype(BF16)
    for c0 in range(0, DIFF_QK, MATMUL_COLS):
        cs = slice(c0, c0 + MATMUL_COLS)
        qt_ref[cs, :] = (_dot_nt(wqt_ref[cs, :], xn) * DIFF_Q_SCALE).astype(BF16)
        vt_ref[cs, :] = _dot_nt(wvt_ref[cs, :], xn).astype(BF16)
    _proj_cols(xn, wk_ref, 0, DIFF_QK, k_ref)


def _qkv_proj(x, gain, wq_t, wk, wv_t, tm):
    t = x.shape[0]
    row = lambda wd: pl.BlockSpec((tm, wd), lambda i: (i, 0))
    tr = pl.BlockSpec((None, DIFF_QK, tm), lambda i: (i, 0, 0))
    tr_shape = jax.ShapeDtypeStruct((t // tm, DIFF_QK, tm), BF16)
    return pl.pallas_call(
        _qkv_proj_kernel,
        grid=(t // tm,),
        in_specs=[row(D_MODEL), _resident((1, D_MODEL)), _resident(wq_t.shape), _resident(wk.shape),
                  _resident(wv_t.shape)],
        out_specs=[tr, row(DIFF_QK), tr],
        out_shape=[tr_shape, jax.ShapeDtypeStruct((t, DIFF_QK), BF16), tr_shape],
        compiler_params=_params(("parallel",)),
        name="qkv_proj",
    )(x, gain, wq_t, wk, wv_t)


def _diff_attn_kernel(slope_ref, qt_ref, k_ref, kx_ref, vt_ref, lam_ref, nw_ref, o_ref,
                      acc_ref, m_ref, l_ref, qa_ref, s_a, s_b, *, lambda_init):
    tile = qt_ref.shape[1]
    h = pl.program_id(1)
    i = pl.program_id(2)
    tile_step = slope_ref[h] * (LOG2E * tile)
    half = DIFF_HEAD_DIM

    qt = qt_ref[...]
    zeros = jnp.zeros((half, tile), BF16)
    ones_rows = (lax.broadcasted_iota(jnp.int32, (LANES, tile), 0) < ALIBI_PIECES).astype(BF16)
    qa_ref[:, :tile] = jnp.concatenate([qt[:half], zeros, ones_rows], axis=0)
    qa_ref[:, tile:] = jnp.concatenate([zeros, qt[half:], ones_rows], axis=0)
    acc_ref[...] = jnp.zeros_like(acc_ref)
    l_ref[...] = jnp.zeros_like(l_ref)
    m_ref[...] = jnp.full_like(m_ref, NEG_BIG)

    def scores(t, s_ref, masked):
        start = pl.multiple_of(t * tile, tile)
        k_aug = jnp.concatenate([k_ref[pl.ds(start, tile), :], kx_ref[...]], axis=1)
        s = _dot(k_aug, qa_ref[...])
        if masked:
            key_pos = lax.broadcasted_iota(jnp.int32, (tile, 2 * tile), 0)
            lane_pos = lax.broadcasted_iota(jnp.int32, (tile, 2 * tile), 1)
            causal = key_pos <= jnp.where(lane_pos >= tile, lane_pos - tile, lane_pos)
            s = jnp.where(causal, s, NEG_BIG)
        s_ref[...] = s

    def softmax_pv(t, s_ref):
        off = -tile_step * (i - t).astype(F32)
        s = s_ref[...]
        m_old = m_ref[0:1, :]
        m_new = jnp.maximum(m_old, jnp.max(s, axis=0, keepdims=True) + off)
        alpha = jnp.exp2(m_old - m_new)
        p = jnp.exp2(s + (off - m_new))
        l_ref[0:1, :] = alpha * l_ref[0:1, :] + jnp.sum(p, axis=0, keepdims=True)
        acc_ref[...] = alpha * acc_ref[...] + _dot(vt_ref[t], p.astype(BF16))
        m_ref[0:1, :] = m_new

    even = i % 2 == 0

    @pl.when(i == 0)
    def _():
        scores(0, s_a, True)

    @pl.when(jnp.logical_and(even, i > 0))
    def _():
        scores(0, s_a, False)
        scores(1, s_b, False)
        softmax_pv(0, s_a)

    @pl.when(jnp.logical_not(even))
    def _():
        scores(0, s_b, False)

    first = jnp.where(even, 1, 0)

    def pair(u, carry):
        t = first + 2 * u
        scores(t + 1, s_a, False)
        softmax_pv(t, s_b)
        scores(t + 2, s_b, False)
        softmax_pv(t + 1, s_a)
        return carry

    lax.fori_loop(0, (i - 1) // 2, pair, 0)

    @pl.when(i > 0)
    def _():
        scores(i, s_a, True)
        softmax_pv(i - 1, s_b)

    softmax_pv(i, s_a)

    lam = (jnp.exp(jnp.sum(lam_ref[0:1, :] * lam_ref[1:2, :], axis=-1, keepdims=True))
           - jnp.exp(jnp.sum(lam_ref[2:3, :] * lam_ref[3:4, :], axis=-1, keepdims=True)) + lambda_init)
    normed = acc_ref[...] / l_ref[0:1, :]
    o = (normed[:, :tile] - lam * normed[:, tile:]).T
    o_ref[...] = (_rms(o, nw_ref[...]) * (1.0 - lambda_init)).astype(o_ref.dtype)


def _alibi_slopes(n):
    start = 2.0 ** (-8.0 / n)
    return np.array([start ** (i + 1) for i in range(n)], dtype=np.float32)


def _alibi_key_columns(tile):
    target = (LOG2E * _alibi_slopes(DIFF_HEADS))[:, None] * np.arange(tile, dtype=np.float32)[None, :]
    rest = jnp.asarray(target, F32)
    pieces = []
    for _ in range(ALIBI_PIECES):
        piece = rest.astype(BF16)
        pieces.append(piece)
        rest = rest - piece.astype(F32)
    cols = jnp.stack(pieces, axis=-1)
    return jnp.pad(cols, ((0, 0), (0, 0), (0, LANES - ALIBI_PIECES)))


def _diff_attention(qt, k, vt, lam_vecs, subln, lambda_init, batch):
    nblk, _, tile = qt.shape
    nq = nblk // batch
    seq = nq * tile
    return pl.pallas_call(
        functools.partial(_diff_attn_kernel, lambda_init=lambda_init),
        grid=(batch, DIFF_HEADS, nq),
        in_specs=[pl.BlockSpec(memory_space=pltpu.SMEM),
                  pl.BlockSpec((None, LANES, tile), lambda b, h, i: (b * nq + i, h, 0)),
                  pl.BlockSpec((seq, LANES), lambda b, h, i: (b, h)),
                  pl.BlockSpec((None, tile, LANES), lambda b, h, i: (h, 0, 0)),
                  pl.BlockSpec((nq, LANES, tile), lambda b, h, i: (b, h, 0)),
                  _resident((4, DIFF_HEAD_DIM)), _resident((1, DIFF_V_DIM))],
        out_specs=pl.BlockSpec((tile, LANES), lambda b, h, i: (b * nq + i, h)),
        out_shape=jax.ShapeDtypeStruct((batch * seq, DIFF_HEADS * DIFF_V_DIM), BF16),
        scratch_shapes=[pltpu.VMEM((DIFF_V_DIM, 2 * tile), F32), pltpu.VMEM((SUBLANES, 2 * tile), F32),
                        pltpu.VMEM((SUBLANES, 2 * tile), F32), pltpu.VMEM((2 * LANES, 2 * tile), BF16),
                        pltpu.VMEM((tile, 2 * tile), F32), pltpu.VMEM((tile, 2 * tile), F32)],
        compiler_params=_params(("arbitrary", "arbitrary", "arbitrary")),
        name="diff_attention",
    )(jnp.asarray(_alibi_slopes(DIFF_HEADS)), qt, k, _alibi_key_columns(tile), vt, lam_vecs, subln[None, :])


def _cross_attn_kernel(x_ref, g_ref, wq_ref, k_ref, v_ref, wo_ref, o_ref, att_buf):
    x = x_ref[...]
    xn = _rms(x, g_ref[...]).astype(BF16)
    for h in range(X_HEADS):
        hs = slice(h * X_HEAD_DIM, (h + 1) * X_HEAD_DIM)
        qh = _dot(xn, wq_ref[:, hs]).astype(BF16)
        s = _dot_nt(qh, k_ref[:, hs]) * (X_HEAD_DIM ** -0.5)
        p = jnp.exp(s - jnp.max(s, axis=-1, keepdims=True))
        p = p / jnp.sum(p, axis=-1, keepdims=True)
        att_buf[:, hs] = _dot(p.astype(BF16), v_ref[:, hs]).astype(BF16)
    for c0 in range(0, D_MODEL, MATMUL_COLS):
        cs = slice(c0, c0 + MATMUL_COLS)
        o_ref[:, cs] = x[:, cs] + _dot(att_buf[...], wo_ref[:, cs])


def _cross_attention(x, gain, wq, k_mem, v_mem, wo, batch):
    t = x.shape[0]
    seq = t // batch
    tm = min(TOKEN_TILE, seq)
    per = seq // tm
    n_mem = k_mem.shape[0] // batch
    row = pl.BlockSpec((tm, D_MODEL), lambda i: (i, 0))
    mem = pl.BlockSpec((n_mem, D_MODEL), lambda i: (i // per, 0))
    return pl.pallas_call(
        _cross_attn_kernel,
        grid=(t // tm,),
        in_specs=[row, _resident((1, D_MODEL)), _resident(wq.shape), mem, mem, _resident(wo.shape)],
        out_specs=row,
        out_shape=jax.ShapeDtypeStruct((t, D_MODEL), F32),
        scratch_shapes=[pltpu.VMEM((tm, D_MODEL), BF16)],
        compiler_params=_params(("parallel",)),
        name="cross_attention",
    )(x, gain, wq, k_mem, v_mem, wo)


def _mlp_kernel(x_ref, g_ref, w1_ref, w2_ref, fg_ref, o_ref, acc_ref, *, final_norm):
    x = x_ref[...]
    xn = _rms(x, g_ref[...]).astype(BF16)
    acc_ref[...] = x
    for c0 in range(0, D_FF, MATMUL_COLS):
        hid = jnp.maximum(_dot(xn, w1_ref[:, c0:c0 + MATMUL_COLS]), 0.0)
        acc_ref[...] += _dot((hid * hid).astype(BF16), w2_ref[c0:c0 + MATMUL_COLS, :])
    out = acc_ref[...]
    if final_norm:
        out = _rms(out, fg_ref[...])
    o_ref[...] = out


def _mlp(x, gain, w1, w2, final_gain, final_norm):
    t = x.shape[0]
    tm = min(TOKEN_TILE, t)
    row = pl.BlockSpec((tm, D_MODEL), lambda i: (i, 0))
    return pl.pallas_call(
        functools.partial(_mlp_kernel, final_norm=final_norm),
        grid=(t // tm,),
        in_specs=[row, _resident((1, D_MODEL)), _resident(w1.shape), _resident(w2.shape), _resident((1, D_MODEL))],
        out_specs=row,
        out_shape=jax.ShapeDtypeStruct((t, D_MODEL), F32),
        scratch_shapes=[pltpu.VMEM((tm, D_MODEL), F32)],
        compiler_params=_params(("parallel",)),
        name="mlp",
    )(x, gain, w1, w2, final_gain)


def _even_weights(w_in, gla_w2):
    z_end = SSD_INNER
    xbc_end = z_end + SSD_XBC
    dt_end = xbc_end + SSD_HEADS
    v_end = dt_end + 2 * GLA_KEY + GLA_VAL
    glr_end = v_end + GLA_RANK
    w_main = jnp.concatenate([w_in[:, :xbc_end], w_in[:, dt_end:v_end], w_in[:, glr_end:]], axis=1).astype(BF16)
    w_small = jnp.concatenate([w_in[:, xbc_end:dt_end], w_in[:, v_end:glr_end],
                               jnp.zeros((D_MODEL, LANES - SSD_HEADS - GLA_RANK), w_in.dtype)], axis=1).astype(BF16)
    w2_pad = jnp.zeros((LANES, GLA_KEY), F32).at[SSD_HEADS:SSD_HEADS + GLA_RANK].set(gla_w2).astype(BF16)
    return w_main, w_small, w2_pad


def kernel(x, mem, ev_norm, ev_w_in, ev_conv_w, ev_conv_b, ev_dt_bias, ev_a_log, ev_d_skip, ev_ssd_norm, ev_gla_w2, ev_gla_b, ev_gla_norm, ev_w_out, od_norm, od_w_qkv, od_lam_q1, od_lam_k1, od_lam_q2, od_lam_k2, od_subln, od_w_o, xa_norm, xa_mem_norm, xa_wq, xa_wkv, xa_wo, mlp_norm, mlp_w1, mlp_w2, final_norm):
    batch, seq, d = x.shape
    n_mem = mem.shape[1]
    n_layers = xa_norm.shape[0]
    xf = x.reshape(batch * seq, d)
    memf = mem.reshape(batch * n_mem, d)
    for layer in range(n_layers):
        i = layer // 2
        if layer % 2 == 0:
            w_main, w_small, w2_pad = _even_weights(ev_w_in[i], ev_gla_w2[i])
            z, xbc, q, k, v, r, small, logg = _even_in_proj(xf, ev_norm[i][None, :], w_main, w_small, w2_pad,
                                                            ev_gla_b[i][None, :])
            y = _ssd_scan(xbc, z, small, ev_conv_w[i], ev_conv_b[i], ev_dt_bias[i], ev_a_log[i], ev_d_skip[i],
                          ev_ssd_norm[i], batch)
            o = _gla_scan(q, k, v, logg, r, ev_gla_norm[i], batch)
            xf = _proj_residual(xf, [y, o], ev_w_out[i].astype(BF16))
        else:
            lambda_init = 0.8 - 0.6 * math.exp(-0.3 * layer)
            w_qkv = od_w_qkv[i].astype(BF16)
            qt, k, vt = _qkv_proj(xf, od_norm[i][None, :], w_qkv[:, :DIFF_QK].T, w_qkv[:, DIFF_QK:2 * DIFF_QK],
                                  w_qkv[:, 2 * DIFF_QK:].T, min(ATTN_TILE, seq))
            lam_vecs = jnp.stack([od_lam_q1[i], od_lam_k1[i], od_lam_q2[i], od_lam_k2[i]]).astype(F32)
            att = _diff_attention(qt, k, vt, lam_vecs, od_subln[i], lambda_init, batch)
            xf = _proj_residual(xf, [att], od_w_o[i].astype(BF16))
        k_mem, v_mem = _norm_proj(memf, xa_mem_norm[layer][None, :], xa_wkv[layer].astype(BF16), (d, d))
        xf = _cross_attention(xf, xa_norm[layer][None, :], xa_wq[layer].astype(BF16), k_mem, v_mem,
                              xa_wo[layer].astype(BF16), batch)
        xf = _mlp(xf, mlp_norm[layer][None, :], mlp_w1[layer].astype(BF16), mlp_w2[layer].astype(BF16),
                  final_norm[None, :], layer == n_layers - 1)
    return xf.reshape(batch, seq, d)
```

```python
import functools
import math

import numpy as np
import jax
import jax.numpy as jnp
from jax import lax
from jax.experimental import pallas as pl
from jax.experimental.pallas import tpu as pltpu

F32 = jnp.float32
BF16 = jnp.bfloat16
EPS = 1e-5
NEG_BIG = -1e30

D_MODEL = 1024
N_LAYERS = 4

SSD_HEADS = 16
SSD_HEAD_DIM = 64
SSD_INNER = SSD_HEADS * SSD_HEAD_DIM
SSD_GROUPS = 2
SSD_STATE = 64
SSD_CONV = 4
SSD_CHUNK = 128
SSD_XBC = SSD_INNER + 2 * SSD_GROUPS * SSD_STATE
SSD_GROUP_WIDTH = SSD_INNER // SSD_GROUPS
SSD_GROUP_HEADS = SSD_HEADS // SSD_GROUPS

GLA_HEADS = 4
GLA_KEY = 512
GLA_VAL = 1024
GLA_HK = GLA_KEY // GLA_HEADS
GLA_HV = GLA_VAL // GLA_HEADS
GLA_RANK = 16
GLA_TAU = 16.0
GLA_CHUNK = 64
GLA_SUB = 16
GLA_SAFE_LOG2 = 100.0

DIFF_HEADS = 8
DIFF_HEAD_DIM = 64
DIFF_V_DIM = 2 * DIFF_HEAD_DIM
DIFF_QK = DIFF_HEADS * 2 * DIFF_HEAD_DIM
LOG2E = 1.4426950408889634
DIFF_Q_SCALE = DIFF_HEAD_DIM ** -0.5 * LOG2E
ALIBI_PIECES = 3
SPLIT_PIECES = 3

X_HEADS = 4
X_HEAD_DIM = D_MODEL // X_HEADS
D_FF = 4 * D_MODEL

LANES = 128
SUBLANES = 8
VMEM_LIMIT = 56 * 1024 * 1024
TOKEN_TILE = 512
ATTN_TILE = 1024
CROSS_TILE = 1024
MATMUL_COLS = 512


def _resident(shape):
    nd = len(shape)
    return pl.BlockSpec(shape, lambda *_: (0,) * nd, pipeline_mode=pl.Buffered(1))


def _params(semantics):
    return pltpu.CompilerParams(dimension_semantics=semantics, vmem_limit_bytes=VMEM_LIMIT)


def _rms(x, gain):
    return x * lax.rsqrt(jnp.mean(x * x, axis=-1, keepdims=True) + EPS) * gain


def _sigmoid(x):
    return 1.0 / (1.0 + jnp.exp(-x))


def _softplus(x):
    return jnp.maximum(x, 0.0) + jnp.log1p(jnp.exp(-jnp.abs(x)))


def _dot(a, b):
    return jnp.dot(a, b, preferred_element_type=F32)


def _dot_nt(a, b):
    return lax.dot_general(a, b, (((1,), (1,)), ((), ())), preferred_element_type=F32)


def _dot_tn(a, b):
    return lax.dot_general(a, b, (((0,), (0,)), ((), ())), preferred_element_type=F32)


def _split3(x):
    pieces = []
    rest = x
    for _ in range(SPLIT_PIECES):
        piece = rest.astype(BF16)
        pieces.append(piece)
        rest = rest - piece.astype(F32)
    return pieces


def _proj_cols(xn, w_ref, off, width, o_ref, scale=None):
    for c0 in range(0, width, MATMUL_COLS):
        cw = min(MATMUL_COLS, width - c0)
        acc = _dot(xn, w_ref[:, off + c0:off + c0 + cw])
        if scale is not None:
            acc = acc * scale
        o_ref[:, c0:c0 + cw] = acc.astype(o_ref.dtype)


_EVEN_MAIN = (SSD_INNER, SSD_XBC, GLA_KEY, GLA_KEY, GLA_VAL, GLA_VAL)


def _even_in_kernel(x_ref, g_ref, w_ref, ws_ref, w2_ref, gb_ref,
                    z_ref, xbc_ref, q_ref, k_ref, v_ref, r_ref, small_ref, logg_ref):
    xn = _rms(x_ref[...], g_ref[...]).astype(BF16)
    off = 0
    for width, o_ref, scale in zip(_EVEN_MAIN, (z_ref, xbc_ref, q_ref, k_ref, v_ref, r_ref),
                                   (None, None, GLA_HK ** -0.5, None, None, None)):
        _proj_cols(xn, w_ref, off, width, o_ref, scale)
        off += width
    small = _dot(xn, ws_ref[...])
    small_ref[...] = small
    logits = _dot(small.astype(BF16), w2_ref[...]) + gb_ref[...]
    logg_ref[...] = -_softplus(-logits) * (LOG2E / GLA_TAU)


def _even_in_proj(x, gain, w_main, w_small, w2_pad, gla_b):
    t = x.shape[0]
    tm = min(TOKEN_TILE, t)
    row = lambda w: pl.BlockSpec((tm, w), lambda i: (i, 0))
    out_shape = [jax.ShapeDtypeStruct((t, w), BF16) for w in _EVEN_MAIN]
    out_shape += [jax.ShapeDtypeStruct((t, LANES), F32), jax.ShapeDtypeStruct((t, GLA_KEY), F32)]
    return pl.pallas_call(
        _even_in_kernel,
        grid=(t // tm,),
        in_specs=[row(D_MODEL), _resident((1, D_MODEL)), _resident(w_main.shape), _resident(w_small.shape),
                  _resident(w2_pad.shape), _resident((1, GLA_KEY))],
        out_specs=[row(w) for w in _EVEN_MAIN] + [row(LANES), row(GLA_KEY)],
        out_shape=out_shape,
        compiler_params=_params(("parallel",)),
        name="even_in_proj",
    )(x, gain, w_main, w_small, w2_pad, gla_b)


def _ssd_kernel(xbc_ref, z_ref, small_ref, cw_ref, cb_ref, dtb_ref, alog_ref, dskip_ref, nw_ref, expand_ref,
                tri_ref, y_ref, conv_buf, state_ref, y_buf):
    q = SSD_CHUNK

    @pl.when(pl.program_id(1) == 0)
    def _():
        conv_buf[0:SUBLANES, :] = jnp.zeros((SUBLANES, SSD_XBC), F32)
        state_ref[...] = jnp.zeros_like(state_ref)

    conv_buf[SUBLANES:SUBLANES + q, :] = xbc_ref[...].astype(F32)
    acc = cb_ref[...]
    for tap in range(SSD_CONV):
        start = SUBLANES - (SSD_CONV - 1) + tap
        acc = acc + cw_ref[tap:tap + 1, :] * conv_buf[start:start + q, :]
    conv_buf[0:SUBLANES, :] = conv_buf[q:q + SUBLANES, :]
    u = acc * _sigmoid(acc)
    xs = u[:, :SSD_INNER]
    bm = u[:, SSD_INNER:SSD_INNER + SSD_GROUPS * SSD_STATE]
    cm = u[:, SSD_INNER + SSD_GROUPS * SSD_STATE:]
    bm_t = bm.T

    lane = lax.broadcasted_iota(jnp.int32, (q, LANES), 1)
    dt = jnp.where(lane < SSD_HEADS, _softplus(small_ref[...] + dtb_ref[...]), 0.0)
    loga = dt * (-jnp.exp(alog_ref[...]) * LOG2E)
    cum = _dot(tri_ref[...], jnp.concatenate(_split3(loga), axis=0))
    cum_row = cum.T
    expand = expand_ref[...]
    dt_full = _dot(jnp.concatenate(_split3(dt), axis=1), expand)
    cum_full = _dot(jnp.concatenate(_split3(cum), axis=1), expand)
    last_full = cum_full[q - 1:q, :]
    xdt = xs * dt_full
    xdt_b = xdt.astype(BF16)
    tril = lax.broadcasted_iota(jnp.int32, (q, q), 0) >= lax.broadcasted_iota(jnp.int32, (q, q), 1)

    for g in range(SSD_GROUPS):
        gs = slice(g * SSD_GROUP_WIDTH, (g + 1) * SSD_GROUP_WIDTH)
        cg = cm[:, g * SSD_STATE:(g + 1) * SSD_STATE].astype(BF16)
        bg = bm[:, g * SSD_STATE:(g + 1) * SSD_STATE].astype(BF16)
        scores = _dot_nt(cg, bg)
        st = state_ref[g]
        y_inter = _dot(cg, st.astype(BF16)) * jnp.exp2(cum_full[:, gs])
        for e in range(SSD_GROUP_HEADS):
            h = g * SSD_GROUP_HEADS + e
            hs = slice(h * SSD_HEAD_DIM, (h + 1) * SSD_HEAD_DIM)
            seg = cum[:, h:h + 1] - cum_row[h:h + 1, :]
            decay = jnp.exp2(jnp.where(tril, seg, NEG_BIG))
            ye = _dot((scores * decay).astype(BF16), xdt_b[:, hs])
            y_buf[:, hs] = ye + y_inter[:, e * SSD_HEAD_DIM:(e + 1) * SSD_HEAD_DIM]
        wts = jnp.exp2(last_full[:, gs] - cum_full[:, gs])
        xw = (xdt[:, gs] * wts).astype(BF16)
        bg_t = bm_t[g * SSD_STATE:(g + 1) * SSD_STATE, :].astype(BF16)
        state_ref[g] = st * jnp.exp2(last_full[:, gs]) + _dot(bg_t, xw)

    zf = z_ref[...].astype(F32)
    y = (y_buf[...] + dskip_ref[...] * xs) * (zf * _sigmoid(zf))
    for g in range(SSD_GROUPS):
        gs = slice(g * SSD_GROUP_WIDTH, (g + 1) * SSD_GROUP_WIDTH)
        y_ref[:, gs] = _rms(y[:, gs], nw_ref[:, gs]).astype(y_ref.dtype)


def _ssd_scan(xbc, z, small, conv_w, conv_b, dt_bias, a_log, d_skip, norm_w, batch):
    t = xbc.shape[0]
    q = SSD_CHUNK
    nc = t // batch // q
    pad = lambda v: jnp.pad(v.astype(F32), (0, LANES - v.shape[0]))[None, :]
    expand = (np.arange(LANES)[:, None] == (np.arange(SSD_INNER)[None, :] // SSD_HEAD_DIM)).astype(np.float32)
    expand = np.tile(expand, (SPLIT_PIECES, 1))
    tri = np.tile(np.tril(np.ones((q, q), np.float32)), (1, SPLIT_PIECES))
    row = lambda w: pl.BlockSpec((q, w), lambda b, c: (b * nc + c, 0))
    return pl.pallas_call(
        _ssd_kernel,
        grid=(batch, nc),
        in_specs=[row(SSD_XBC), row(SSD_INNER), row(LANES), _resident((SSD_CONV, SSD_XBC)), _resident((1, SSD_XBC)),
                  _resident((1, LANES)), _resident((1, LANES)), _resident((1, SSD_INNER)), _resident((1, SSD_INNER)),
                  _resident((SPLIT_PIECES * LANES, SSD_INNER)), _resident((q, SPLIT_PIECES * q))],
        out_specs=row(SSD_INNER),
        out_shape=jax.ShapeDtypeStruct((t, SSD_INNER), BF16),
        scratch_shapes=[pltpu.VMEM((q + SUBLANES, SSD_XBC), F32),
                        pltpu.VMEM((SSD_GROUPS, SSD_STATE, SSD_GROUP_WIDTH), F32),
                        pltpu.VMEM((q, SSD_INNER), F32)],
        compiler_params=_params(("arbitrary", "arbitrary")),
        name="ssd_scan",
    )(xbc, z, small, conv_w, conv_b[None, :], pad(dt_bias), pad(a_log),
      jnp.repeat(d_skip.astype(F32), SSD_HEAD_DIM)[None, :], norm_w[None, :], jnp.asarray(expand, BF16),
      jnp.asarray(tri, BF16))


def _gla_kernel(q_ref, k_ref, v_ref, g_ref, r_ref, nw_ref, tri_ref, o_ref,
                state_ref, k_pad, c_pad, v_pad, term_buf, intra_buf):
    n = GLA_CHUNK
    sub = GLA_SUB
    nsub = n // sub
    heads = [(slice(h * GLA_HK, (h + 1) * GLA_HK), slice(h * GLA_HV, (h + 1) * GLA_HV)) for h in range(GLA_HEADS)]

    @pl.when(pl.program_id(1) == 0)
    def _():
        state_ref[...] = jnp.zeros_like(state_ref)

    row = lax.broadcasted_iota(jnp.int32, (n, n), 0)
    col = lax.broadcasted_iota(jnp.int32, (n, n), 1)
    cums = [_dot(tri_ref[...], jnp.concatenate(_split3(g_ref[:, ks]), axis=0)) for ks, _ in heads]

    def sub_ref(cum, i):
        return cum[i * sub - 1:i * sub, :] if i > 0 else jnp.zeros((1, GLA_HK), F32)

    def factored_blocks(h, first, clamp):
        ks, _ = heads[h]
        cum = cums[h]
        qh = q_ref[:, ks].astype(F32)
        kh = k_ref[:, ks].astype(F32)
        blocks = [jnp.zeros((sub, n), F32)] * first
        for i in range(first, nsub):
            ref = sub_ref(cum, i)
            qt = qh[i * sub:(i + 1) * sub, :] * jnp.exp2(cum[i * sub:(i + 1) * sub, :] - ref)
            kt = kh * jnp.exp2(jnp.minimum(ref - cum, clamp))
            blocks.append(_dot_nt(qt.astype(BF16), kt.astype(BF16)))
        return jnp.concatenate(blocks, axis=0)

    worst = jnp.max(jnp.concatenate(
        [sub_ref(cum, i) - cum[(i + 1) * sub - 1:(i + 1) * sub, :] for cum in cums for i in range(nsub)], axis=0))
    safe = worst <= GLA_SAFE_LOG2

    @pl.when(safe)
    def _():
        for h, (_, vs) in enumerate(heads):
            a = jnp.where(row >= col, factored_blocks(h, 0, GLA_SAFE_LOG2), 0.0)
            intra_buf[:, vs] = _dot(a.astype(BF16), v_ref[:, vs])

    @pl.when(jnp.logical_not(safe))
    def _():
        k_pad[0:sub, :] = jnp.zeros((sub, GLA_HK), F32)
        c_pad[0:sub, :] = jnp.zeros((sub, GLA_HK), F32)
        v_pad[0:sub, :] = jnp.zeros((sub, GLA_HV), F32)
        ones = jnp.ones((GLA_HK, GLA_HV), BF16)
        for h, (ks, vs) in enumerate(heads):
            cum = cums[h]
            qh = q_ref[:, ks].astype(F32)
            vh_b = v_ref[:, vs]
            a_far = jnp.where((row - col) >= sub, factored_blocks(h, 1, 0.0), 0.0)
            o = _dot(a_far.astype(BF16), vh_b)
            k_pad[sub:sub + n, :] = k_ref[:, ks].astype(F32)
            c_pad[sub:sub + n, :] = cum
            v_pad[sub:sub + n, :] = vh_b.astype(F32)
            for d in range(sub):
                kd = k_pad[sub - d:sub - d + n, :]
                cd = c_pad[sub - d:sub - d + n, :]
                term_buf[d * n:(d + 1) * n, :] = (qh * kd * jnp.exp2(cum - cd)).astype(BF16)
            near = _dot(term_buf[...], ones)
            for d in range(sub):
                o = o + near[d * n:(d + 1) * n, :] * v_pad[sub - d:sub - d + n, :]
            intra_buf[:, vs] = o

    for h, (ks, vs) in enumerate(heads):
        cum = cums[h]
        last = cum[n - 1:n, :]
        qh = q_ref[:, ks].astype(F32)
        kh = k_ref[:, ks].astype(F32)
        vh_b = v_ref[:, vs]
        st = state_ref[h]
        o = intra_buf[:, vs] + _dot_nt((qh * jnp.exp2(cum)).astype(BF16), st.astype(BF16))
        state_ref[h] = st * jnp.exp2(last) + _dot_tn(vh_b, (kh * jnp.exp2(last - cum)).astype(BF16))
        rf = r_ref[:, vs].astype(F32)
        o_ref[:, vs] = (_rms(o, nw_ref[...]) * (rf * _sigmoid(rf))).astype(o_ref.dtype)


def _gla_scan(q, k, v, logg, r, norm_w, batch):
    t = q.shape[0]
    n = GLA_CHUNK
    nc = t // batch // n
    tri = np.tile(np.tril(np.ones((n, n), np.float32)), (1, SPLIT_PIECES))
    row = lambda w: pl.BlockSpec((n, w), lambda b, c: (b * nc + c, 0))
    return pl.pallas_call(
        _gla_kernel,
        grid=(batch, nc),
        in_specs=[row(GLA_KEY), row(GLA_KEY), row(GLA_VAL), row(GLA_KEY), row(GLA_VAL),
                  _resident((1, GLA_HV)), _resident((n, SPLIT_PIECES * n))],
        out_specs=row(GLA_VAL),
        out_shape=jax.ShapeDtypeStruct((t, GLA_VAL), BF16),
        scratch_shapes=[pltpu.VMEM((GLA_HEADS, GLA_HV, GLA_HK), F32),
                        pltpu.VMEM((n + GLA_SUB, GLA_HK), F32),
                        pltpu.VMEM((n + GLA_SUB, GLA_HK), F32),
                        pltpu.VMEM((n + GLA_SUB, GLA_HV), F32),
                        pltpu.VMEM((GLA_SUB * n, GLA_HK), BF16),
                        pltpu.VMEM((n, GLA_VAL), F32)],
        compiler_params=_params(("arbitrary", "arbitrary")),
        name="gla_scan",
    )(q, k, v, logg, r, norm_w[None, :], jnp.asarray(tri, BF16))


def _proj_residual_kernel(*refs, widths):
    x_ref = refs[0]
    a_refs = refs[1:1 + len(widths)]
    w_ref = refs[1 + len(widths)]
    o_ref = refs[2 + len(widths)]
    for c0 in range(0, D_MODEL, MATMUL_COLS):
        cs = slice(c0, c0 + MATMUL_COLS)
        acc = x_ref[:, cs]
        off = 0
        for a_ref, width in zip(a_refs, widths):
            acc = acc + _dot(a_ref[...], w_ref[off:off + width, cs])
            off += width
        o_ref[:, cs] = acc


def _proj_residual(x, acts, w):
    t = x.shape[0]
    tm = min(TOKEN_TILE, t)
    widths = tuple(a.shape[1] for a in acts)
    row = lambda wd: pl.BlockSpec((tm, wd), lambda i: (i, 0))
    return pl.pallas_call(
        functools.partial(_proj_residual_kernel, widths=widths),
        grid=(t // tm,),
        in_specs=[row(D_MODEL)] + [row(wd) for wd in widths] + [_resident(w.shape)],
        out_specs=row(D_MODEL),
        out_shape=jax.ShapeDtypeStruct((t, D_MODEL), F32),
        compiler_params=_params(("parallel",)),
        name="proj_residual",
    )(x, *acts, w)


def _norm_proj_kernel(x_ref, g_ref, w_ref, *o_refs):
    xn = _rms(x_ref[...], g_ref[...]).astype(BF16)
    off = 0
    for o_ref in o_refs:
        width = o_ref.shape[1]
        _proj_cols(xn, w_ref, off, width, o_ref)
        off += width


def _norm_proj(x, gain, w, widths):
    t = x.shape[0]
    tm = min(TOKEN_TILE, t)
    row = lambda wd: pl.BlockSpec((tm, wd), lambda i: (i, 0))
    return pl.pallas_call(
        _norm_proj_kernel,
        grid=(t // tm,),
        in_specs=[row(D_MODEL), _resident((1, D_MODEL)), _resident(w.shape)],
        out_specs=[row(wd) for wd in widths],
        out_shape=[jax.ShapeDtypeStruct((t, wd), BF16) for wd in widths],
        compiler_params=_params(("parallel",)),
        name="norm_proj",
    )(x, gain, w)


def _qkv_proj_kernel(x_ref, g_ref, wqt_ref, wk_ref, wvt_ref, qt_ref, k_ref, vt_ref):
    xn = _rms(x_ref[...], g_ref[...]).astype(BF16)
    for c0 in range(0, DIFF_QK, MATMUL_COLS):
        cs = slice(c0, c0 + MATMUL_COLS)
        qt_ref[cs, :] = (_dot_nt(wqt_ref[cs, :], xn) * DIFF_Q_SCALE).astype(BF16)
        vt_ref[cs, :] = _dot_nt(wvt_ref[cs, :], xn).astype(BF16)
    _proj_cols(xn, wk_ref, 0, DIFF_QK, k_ref)


def _qkv_proj(x, gain, wq_t, wk, wv_t, tm):
    t = x.shape[0]
    row = lambda wd: pl.BlockSpec((tm, wd), lambda i: (i, 0))
    tr = pl.BlockSpec((None, DIFF_QK, tm), lambda i: (i, 0, 0))
    tr_shape = jax.ShapeDtypeStruct((t // tm, DIFF_QK, tm), BF16)
    return pl.pallas_call(
        _qkv_proj_kernel,
        grid=(t // tm,),
        in_specs=[row(D_MODEL), _resident((1, D_MODEL)), _resident(wq_t.shape), _resident(wk.shape),
                  _resident(wv_t.shape)],
        out_specs=[tr, row(DIFF_QK), tr],
        out_shape=[tr_shape, jax.ShapeDtypeStruct((t, DIFF_QK), BF16), tr_shape],
        compiler_params=_params(("parallel",)),
        name="qkv_proj",
    )(x, gain, wq_t, wk, wv_t)


def _diff_attn_kernel(slope_ref, qt_ref, k_ref, kx_ref, vt_ref, lam_ref, nw_ref, o_ref,
                      acc_ref, m_ref, l_ref, qa_ref, s_a, s_b, *, lambda_init):
    tile = qt_ref.shape[1]
    h = pl.program_id(1)
    i = pl.program_id(2)
    tile_step = slope_ref[h] * (LOG2E * tile)
    half = DIFF_HEAD_DIM

    qt = qt_ref[...]
    zeros = jnp.zeros((half, tile), BF16)
    ones_rows = (lax.broadcasted_iota(jnp.int32, (LANES, tile), 0) < ALIBI_PIECES).astype(BF16)
    qa_ref[:, :tile] = jnp.concatenate([qt[:half], zeros, ones_rows], axis=0)
    qa_ref[:, tile:] = jnp.concatenate([zeros, qt[half:], ones_rows], axis=0)
    acc_ref[...] = jnp.zeros_like(acc_ref)
    l_ref[...] = jnp.zeros_like(l_ref)
    m_ref[...] = jnp.full_like(m_ref, NEG_BIG)

    def scores(t, s_ref, masked):
        start = pl.multiple_of(t * tile, tile)
        k_aug = jnp.concatenate([k_ref[pl.ds(start, tile), :], kx_ref[...]], axis=1)
        s = _dot(k_aug, qa_ref[...])
        if masked:
            key_pos = lax.broadcasted_iota(jnp.int32, (tile, 2 * tile), 0)
            lane_pos = lax.broadcasted_iota(jnp.int32, (tile, 2 * tile), 1)
            causal = key_pos <= jnp.where(lane_pos >= tile, lane_pos - tile, lane_pos)
            s = jnp.where(causal, s, NEG_BIG)
        s_ref[...] = s

    def softmax_pv(t, s_ref):
        off = -tile_step * (i - t).astype(F32)
        s = s_ref[...]
        m_old = m_ref[0:1, :]
        m_new = jnp.maximum(m_old, jnp.max(s, axis=0, keepdims=True) + off)
        alpha = jnp.exp2(m_old - m_new)
        p = jnp.exp2(s + (off - m_new))
        l_ref[0:1, :] = alpha * l_ref[0:1, :] + jnp.sum(p, axis=0, keepdims=True)
        acc_ref[...] = alpha * acc_ref[...] + _dot(vt_ref[t], p.astype(BF16))
        m_ref[0:1, :] = m_new

    buf_a = s_a
    buf_b = s_b
    even = i % 2 == 0

    @pl.when(i == 0)
    def _():
        scores(0, buf_a, True)

    @pl.when(jnp.logical_and(even, i > 0))
    def _():
        scores(0, buf_a, False)
        scores(1, buf_b, False)
        softmax_pv(0, buf_a)

    @pl.when(jnp.logical_not(even))
    def _():
        scores(0, buf_b, False)

    first = jnp.where(even, 1, 0)

    def pair(u, carry):
        t = first + 2 * u
        scores(t + 1, buf_a, False)
        softmax_pv(t, buf_b)
        scores(t + 2, buf_b, False)
        softmax_pv(t + 1, buf_a)
        return carry

    lax.fori_loop(0, (i - 1) // 2, pair, 0)

    @pl.when(i > 0)
    def _():
        scores(i, buf_a, True)
        softmax_pv(i - 1, buf_b)

    softmax_pv(i, buf_a)

    lam = (jnp.exp(jnp.sum(lam_ref[0:1, :] * lam_ref[1:2, :], axis=-1, keepdims=True))
           - jnp.exp(jnp.sum(lam_ref[2:3, :] * lam_ref[3:4, :], axis=-1, keepdims=True)) + lambda_init)
    normed = acc_ref[...] / l_ref[0:1, :]
    o = (normed[:, :tile] - lam * normed[:, tile:]).T
    o_ref[...] = (_rms(o, nw_ref[...]) * (1.0 - lambda_init)).astype(o_ref.dtype)


def _alibi_slopes(n):
    start = 2.0 ** (-8.0 / n)
    return np.array([start ** (i + 1) for i in range(n)], dtype=np.float32)


def _alibi_key_columns(tile):
    target = (LOG2E * _alibi_slopes(DIFF_HEADS))[:, None] * np.arange(tile, dtype=np.float32)[None, :]
    rest = jnp.asarray(target, F32)
    pieces = []
    for _ in range(ALIBI_PIECES):
        piece = rest.astype(BF16)
        pieces.append(piece)
        rest = rest - piece.astype(F32)
    cols = jnp.stack(pieces, axis=-1)
    return jnp.pad(cols, ((0, 0), (0, 0), (0, LANES - ALIBI_PIECES)))


def _diff_attention(qt, k, vt, lam_vecs, subln, lambda_init, batch):
    nblk, _, tile = qt.shape
    nq = nblk // batch
    seq = nq * tile
    return pl.pallas_call(
        functools.partial(_diff_attn_kernel, lambda_init=lambda_init),
        grid=(batch, DIFF_HEADS, nq),
        in_specs=[pl.BlockSpec(memory_space=pltpu.SMEM),
                  pl.BlockSpec((None, LANES, tile), lambda b, h, i: (b * nq + i, h, 0)),
                  pl.BlockSpec((seq, LANES), lambda b, h, i: (b, h)),
                  pl.BlockSpec((None, tile, LANES), lambda b, h, i: (h, 0, 0)),
                  pl.BlockSpec((nq, LANES, tile), lambda b, h, i: (b, h, 0)),
                  _resident((4, DIFF_HEAD_DIM)), _resident((1, DIFF_V_DIM))],
        out_specs=pl.BlockSpec((tile, LANES), lambda b, h, i: (b * nq + i, h)),
        out_shape=jax.ShapeDtypeStruct((batch * seq, DIFF_HEADS * DIFF_V_DIM), BF16),
        scratch_shapes=[pltpu.VMEM((DIFF_V_DIM, 2 * tile), F32), pltpu.VMEM((SUBLANES, 2 * tile), F32),
                        pltpu.VMEM((SUBLANES, 2 * tile), F32), pltpu.VMEM((2 * LANES, 2 * tile), BF16),
                        pltpu.VMEM((tile, 2 * tile), F32), pltpu.VMEM((tile, 2 * tile), F32)],
        compiler_params=_params(("arbitrary", "arbitrary", "arbitrary")),
        name="diff_attention",
    )(jnp.asarray(_alibi_slopes(DIFF_HEADS)), qt, k, _alibi_key_columns(tile), vt, lam_vecs, subln[None, :])


def _cross_attn_kernel(x_ref, g_ref, wq_ref, k_ref, v_ref, wo_ref, o_ref, q_buf, att_buf):
    heads = [slice(h * X_HEAD_DIM, (h + 1) * X_HEAD_DIM) for h in range(X_HEADS)]
    xn = _rms(x_ref[...], g_ref[...]).astype(BF16)
    for c0 in range(0, D_MODEL, MATMUL_COLS):
        cs = slice(c0, c0 + MATMUL_COLS)
        q_buf[:, cs] = _dot(xn, wq_ref[:, cs]).astype(BF16)
    scores = [_dot_nt(q_buf[:, hs], k_ref[:, hs]) * (X_HEAD_DIM ** -0.5) for hs in heads]
    for hs, s in zip(heads, scores):
        p = jnp.exp(s - jnp.max(s, axis=-1, keepdims=True))
        pv = _dot(p.astype(BF16), v_ref[:, hs])
        att_buf[:, hs] = (pv / jnp.sum(p, axis=-1, keepdims=True)).astype(BF16)
    for c0 in range(0, D_MODEL, MATMUL_COLS):
        cs = slice(c0, c0 + MATMUL_COLS)
        o_ref[:, cs] = x_ref[:, cs] + _dot(att_buf[...], wo_ref[:, cs])


def _cross_attention(x, gain, wq, k_mem, v_mem, wo, batch):
    t = x.shape[0]
    seq = t // batch
    tm = min(CROSS_TILE, seq)
    per = seq // tm
    n_mem = k_mem.shape[0] // batch
    row = pl.BlockSpec((tm, D_MODEL), lambda i: (i, 0))
    mem = pl.BlockSpec((n_mem, D_MODEL), lambda i: (i // per, 0))
    return pl.pallas_call(
        _cross_attn_kernel,
        grid=(t // tm,),
        in_specs=[row, _resident((1, D_MODEL)), _resident(wq.shape), mem, mem, _resident(wo.shape)],
        out_specs=row,
        out_shape=jax.ShapeDtypeStruct((t, D_MODEL), F32),
        scratch_shapes=[pltpu.VMEM((tm, D_MODEL), BF16), pltpu.VMEM((tm, D_MODEL), BF16)],
        compiler_params=_params(("parallel",)),
        name="cross_attention",
    )(x, gain, wq, k_mem, v_mem, wo)


def _mlp_kernel(x_ref, g_ref, w1_ref, w2_ref, fg_ref, o_ref, acc_ref, *, final_norm):
    x = x_ref[...]
    xn = _rms(x, g_ref[...]).astype(BF16)
    acc_ref[...] = x
    for c0 in range(0, D_FF, MATMUL_COLS):
        hid = jnp.maximum(_dot(xn, w1_ref[:, c0:c0 + MATMUL_COLS]), 0.0)
        acc_ref[...] += _dot((hid * hid).astype(BF16), w2_ref[c0:c0 + MATMUL_COLS, :])
    out = acc_ref[...]
    if final_norm:
        out = _rms(out, fg_ref[...])
    o_ref[...] = out


def _mlp(x, gain, w1, w2, final_gain, final_norm):
    t = x.shape[0]
    tm = min(TOKEN_TILE, t)
    row = pl.BlockSpec((tm, D_MODEL), lambda i: (i, 0))
    return pl.pallas_call(
        functools.partial(_mlp_kernel, final_norm=final_norm),
        grid=(t // tm,),
        in_specs=[row, _resident((1, D_MODEL)), _resident(w1.shape), _resident(w2.shape), _resident((1, D_MODEL))],
        out_specs=row,
        out_shape=jax.ShapeDtypeStruct((t, D_MODEL), F32),
        scratch_shapes=[pltpu.VMEM((tm, D_MODEL), F32)],
        compiler_params=_params(("parallel",)),
        name="mlp",
    )(x, gain, w1, w2, final_gain)


def _even_weights(w_in, gla_w2):
    z_end = SSD_INNER
    xbc_end = z_end + SSD_XBC
    dt_end = xbc_end + SSD_HEADS
    v_end = dt_end + 2 * GLA_KEY + GLA_VAL
    glr_end = v_end + GLA_RANK
    w_main = jnp.concatenate([w_in[:, :xbc_end], w_in[:, dt_end:v_end], w_in[:, glr_end:]], axis=1).astype(BF16)
    w_small = jnp.concatenate([w_in[:, xbc_end:dt_end], w_in[:, v_end:glr_end],
                               jnp.zeros((D_MODEL, LANES - SSD_HEADS - GLA_RANK), w_in.dtype)], axis=1).astype(BF16)
    w2_pad = jnp.zeros((LANES, GLA_KEY), F32).at[SSD_HEADS:SSD_HEADS + GLA_RANK].set(gla_w2).astype(BF16)
    return w_main, w_small, w2_pad


def kernel(x, mem, ev_norm, ev_w_in, ev_conv_w, ev_conv_b, ev_dt_bias, ev_a_log, ev_d_skip, ev_ssd_norm, ev_gla_w2, ev_gla_b, ev_gla_norm, ev_w_out, od_norm, od_w_qkv, od_lam_q1, od_lam_k1, od_lam_q2, od_lam_k2, od_subln, od_w_o, xa_norm, xa_mem_norm, xa_wq, xa_wkv, xa_wo, mlp_norm, mlp_w1, mlp_w2, final_norm):
    batch, seq, d = x.shape
    n_mem = mem.shape[1]
    n_layers = xa_norm.shape[0]
    xf = x.reshape(batch * seq, d)
    memf = mem.reshape(batch * n_mem, d)
    for layer in range(n_layers):
        i = layer // 2
        if layer % 2 == 0:
            w_main, w_small, w2_pad = _even_weights(ev_w_in[i], ev_gla_w2[i])
            z, xbc, q, k, v, r, small, logg = _even_in_proj(xf, ev_norm[i][None, :], w_main, w_small, w2_pad,
                                                            ev_gla_b[i][None, :])
            y = _ssd_scan(xbc, z, small, ev_conv_w[i], ev_conv_b[i], ev_dt_bias[i], ev_a_log[i], ev_d_skip[i],
                          ev_ssd_norm[i], batch)
            o = _gla_scan(q, k, v, logg, r, ev_gla_norm[i], batch)
            xf = _proj_residual(xf, [y, o], ev_w_out[i].astype(BF16))
        else:
            lambda_init = 0.8 - 0.6 * math.exp(-0.3 * layer)
            w_qkv = od_w_qkv[i].astype(BF16)
            qt, k, vt = _qkv_proj(xf, od_norm[i][None, :], w_qkv[:, :DIFF_QK].T, w_qkv[:, DIFF_QK:2 * DIFF_QK],
                                  w_qkv[:, 2 * DIFF_QK:].T, min(ATTN_TILE, seq))
            lam_vecs = jnp.stack([od_lam_q1[i], od_lam_k1[i], od_lam_q2[i], od_lam_k2[i]]).astype(F32)
            att = _diff_attention(qt, k, vt, lam_vecs, od_subln[i], lambda_init, batch)
            xf = _proj_residual(xf, [att], od_w_o[i].astype(BF16))
        k_mem, v_mem = _norm_proj(memf, xa_mem_norm[layer][None, :], xa_wkv[layer].astype(BF16), (d, d))
        xf = _cross_attention(xf, xa_norm[layer][None, :], xa_wq[layer].astype(BF16), k_mem, v_mem,
                              xa_wo[layer].astype(BF16), batch)
        xf = _mlp(xf, mlp_norm[layer][None, :], mlp_w1[layer].astype(BF16), mlp_w2[layer].astype(BF16),
                  final_norm[None, :], layer == n_layers - 1)
    return xf.reshape(batch, seq, d)
```

```python
import functools
import math

import numpy as np
import jax
import jax.numpy as jnp
from jax import lax
from jax.experimental import pallas as pl
from jax.experimental.pallas import tpu as pltpu

F32 = jnp.float32
BF16 = jnp.bfloat16
EPS = 1e-5
NEG_BIG = -1e30

D_MODEL = 1024
N_LAYERS = 4

SSD_HEADS = 16
SSD_HEAD_DIM = 64
SSD_INNER = SSD_HEADS * SSD_HEAD_DIM
SSD_GROUPS = 2
SSD_STATE = 64
SSD_CONV = 4
SSD_CHUNK = 128
SSD_XBC = SSD_INNER + 2 * SSD_GROUPS * SSD_STATE
SSD_GROUP_WIDTH = SSD_INNER // SSD_GROUPS
SSD_GROUP_HEADS = SSD_HEADS // SSD_GROUPS

GLA_HEADS = 4
GLA_KEY = 512
GLA_VAL = 1024
GLA_HK = GLA_KEY // GLA_HEADS
GLA_HV = GLA_VAL // GLA_HEADS
GLA_RANK = 16
GLA_TAU = 16.0
GLA_CHUNK = 64
GLA_SUB = 16
GLA_SAFE_LOG2 = 100.0

DIFF_HEADS = 8
DIFF_HEAD_DIM = 64
DIFF_V_DIM = 2 * DIFF_HEAD_DIM
DIFF_QK = DIFF_HEADS * 2 * DIFF_HEAD_DIM
LOG2E = 1.4426950408889634
DIFF_Q_SCALE = DIFF_HEAD_DIM ** -0.5 * LOG2E
ALIBI_PIECES = 3
DENOM_ROWS = 16
SPLIT_PIECES = 3

X_HEADS = 4
X_HEAD_DIM = D_MODEL // X_HEADS
D_FF = 4 * D_MODEL

LANES = 128
SUBLANES = 8
VMEM_LIMIT = 56 * 1024 * 1024
TOKEN_TILE = 512
ATTN_TILE = 1024
CROSS_TILE = 1024
MATMUL_COLS = 512


def _resident(shape):
    nd = len(shape)
    return pl.BlockSpec(shape, lambda *_: (0,) * nd, pipeline_mode=pl.Buffered(1))


def _params(semantics):
    return pltpu.CompilerParams(dimension_semantics=semantics, vmem_limit_bytes=VMEM_LIMIT)


def _rms(x, gain):
    return x * lax.rsqrt(jnp.mean(x * x, axis=-1, keepdims=True) + EPS) * gain


def _sigmoid(x):
    return 1.0 / (1.0 + jnp.exp(-x))


def _softplus(x):
    return jnp.maximum(x, 0.0) + jnp.log1p(jnp.exp(-jnp.abs(x)))


def _dot(a, b):
    return jnp.dot(a, b, preferred_element_type=F32)


def _dot_nt(a, b):
    return lax.dot_general(a, b, (((1,), (1,)), ((), ())), preferred_element_type=F32)


def _dot_tn(a, b):
    return lax.dot_general(a, b, (((0,), (0,)), ((), ())), preferred_element_type=F32)


def _split3(x):
    pieces = []
    rest = x
    for _ in range(SPLIT_PIECES):
        piece = rest.astype(BF16)
        pieces.append(piece)
        rest = rest - piece.astype(F32)
    return pieces


def _proj_cols(xn, w_ref, off, width, o_ref, scale=None):
    for c0 in range(0, width, MATMUL_COLS):
        cw = min(MATMUL_COLS, width - c0)
        acc = _dot(xn, w_ref[:, off + c0:off + c0 + cw])
        if scale is not None:
            acc = acc * scale
        o_ref[:, c0:c0 + cw] = acc.astype(o_ref.dtype)


_EVEN_MAIN = (SSD_INNER, SSD_XBC, GLA_KEY, GLA_KEY, GLA_VAL, GLA_VAL)


def _even_in_kernel(x_ref, g_ref, w_ref, ws_ref, w2_ref, gb_ref,
                    z_ref, xbc_ref, q_ref, k_ref, v_ref, r_ref, small_ref, logg_ref):
    xn = _rms(x_ref[...], g_ref[...]).astype(BF16)
    off = 0
    for width, o_ref, scale in zip(_EVEN_MAIN, (z_ref, xbc_ref, q_ref, k_ref, v_ref, r_ref),
                                   (None, None, GLA_HK ** -0.5, None, None, None)):
        _proj_cols(xn, w_ref, off, width, o_ref, scale)
        off += width
    small = _dot(xn, ws_ref[...])
    small_ref[...] = small
    logits = _dot(small.astype(BF16), w2_ref[...]) + gb_ref[...]
    logg_ref[...] = -_softplus(-logits) * (LOG2E / GLA_TAU)


def _even_in_proj(x, gain, w_main, w_small, w2_pad, gla_b):
    t = x.shape[0]
    tm = min(TOKEN_TILE, t)
    row = lambda w: pl.BlockSpec((tm, w), lambda i: (i, 0))
    out_shape = [jax.ShapeDtypeStruct((t, w), BF16) for w in _EVEN_MAIN]
    out_shape += [jax.ShapeDtypeStruct((t, LANES), F32), jax.ShapeDtypeStruct((t, GLA_KEY), F32)]
    return pl.pallas_call(
        _even_in_kernel,
        grid=(t // tm,),
        in_specs=[row(D_MODEL), _resident((1, D_MODEL)), _resident(w_main.shape), _resident(w_small.shape),
                  _resident(w2_pad.shape), _resident((1, GLA_KEY))],
        out_specs=[row(w) for w in _EVEN_MAIN] + [row(LANES), row(GLA_KEY)],
        out_shape=out_shape,
        compiler_params=_params(("parallel",)),
        name="even_in_proj",
    )(x, gain, w_main, w_small, w2_pad, gla_b)


def _ssd_kernel(xbc_ref, z_ref, small_ref, cw_ref, cb_ref, dtb_ref, alog_ref, dskip_ref, nw_ref, expand_ref,
                tri_ref, y_ref, conv_buf, state_ref, y_buf, *, chunks):
    @pl.when(pl.program_id(1) == 0)
    def _():
        conv_buf[0:SUBLANES, :] = jnp.zeros((SUBLANES, SSD_XBC), F32)
        state_ref[...] = jnp.zeros_like(state_ref)

    for c in range(chunks):
        rows = pl.ds(c * SSD_CHUNK, SSD_CHUNK)
        _ssd_chunk(xbc_ref.at[rows], z_ref.at[rows], small_ref.at[rows], cw_ref, cb_ref, dtb_ref, alog_ref,
                   dskip_ref, nw_ref, expand_ref, tri_ref, y_ref.at[rows], conv_buf, state_ref, y_buf)


def _ssd_chunk(xbc_ref, z_ref, small_ref, cw_ref, cb_ref, dtb_ref, alog_ref, dskip_ref, nw_ref, expand_ref,
               tri_ref, y_ref, conv_buf, state_ref, y_buf):
    q = SSD_CHUNK
    conv_buf[SUBLANES:SUBLANES + q, :] = xbc_ref[...].astype(F32)
    acc = cb_ref[...]
    for tap in range(SSD_CONV):
        start = SUBLANES - (SSD_CONV - 1) + tap
        acc = acc + cw_ref[tap:tap + 1, :] * conv_buf[start:start + q, :]
    conv_buf[0:SUBLANES, :] = conv_buf[q:q + SUBLANES, :]
    u = acc * _sigmoid(acc)
    xs = u[:, :SSD_INNER]
    bm = u[:, SSD_INNER:SSD_INNER + SSD_GROUPS * SSD_STATE]
    cm = u[:, SSD_INNER + SSD_GROUPS * SSD_STATE:]
    bm_t = bm.T

    lane = lax.broadcasted_iota(jnp.int32, (q, LANES), 1)
    dt = jnp.where(lane < SSD_HEADS, _softplus(small_ref[...] + dtb_ref[...]), 0.0)
    loga = dt * (-jnp.exp(alog_ref[...]) * LOG2E)
    cum = _dot(tri_ref[...], jnp.concatenate(_split3(loga), axis=0))
    cum_row = cum.T
    expand = expand_ref[...]
    dt_full = _dot(jnp.concatenate(_split3(dt), axis=1), expand)
    cum_full = _dot(jnp.concatenate(_split3(cum), axis=1), expand)
    last_full = cum_full[q - 1:q, :]
    xdt = xs * dt_full
    xdt_b = xdt.astype(BF16)
    tril = lax.broadcasted_iota(jnp.int32, (q, q), 0) >= lax.broadcasted_iota(jnp.int32, (q, q), 1)

    for g in range(SSD_GROUPS):
        gs = slice(g * SSD_GROUP_WIDTH, (g + 1) * SSD_GROUP_WIDTH)
        cg = cm[:, g * SSD_STATE:(g + 1) * SSD_STATE].astype(BF16)
        bg = bm[:, g * SSD_STATE:(g + 1) * SSD_STATE].astype(BF16)
        scores = _dot_nt(cg, bg)
        st = state_ref[g]
        y_inter = _dot(cg, st.astype(BF16)) * jnp.exp2(cum_full[:, gs])
        for e in range(SSD_GROUP_HEADS):
            h = g * SSD_GROUP_HEADS + e
            hs = slice(h * SSD_HEAD_DIM, (h + 1) * SSD_HEAD_DIM)
            seg = cum[:, h:h + 1] - cum_row[h:h + 1, :]
            decay = jnp.exp2(jnp.where(tril, seg, NEG_BIG))
            ye = _dot((scores * decay).astype(BF16), xdt_b[:, hs])
            y_buf[:, hs] = ye + y_inter[:, e * SSD_HEAD_DIM:(e + 1) * SSD_HEAD_DIM]
        wts = jnp.exp2(last_full[:, gs] - cum_full[:, gs])
        xw = (xdt[:, gs] * wts).astype(BF16)
        bg_t = bm_t[g * SSD_STATE:(g + 1) * SSD_STATE, :].astype(BF16)
        state_ref[g] = st * jnp.exp2(last_full[:, gs]) + _dot(bg_t, xw)

    zf = z_ref[...].astype(F32)
    y = (y_buf[...] + dskip_ref[...] * xs) * (zf * _sigmoid(zf))
    for g in range(SSD_GROUPS):
        gs = slice(g * SSD_GROUP_WIDTH, (g + 1) * SSD_GROUP_WIDTH)
        y_ref[:, gs] = _rms(y[:, gs], nw_ref[:, gs]).astype(y_ref.dtype)


def _ssd_scan(xbc, z, small, conv_w, conv_b, dt_bias, a_log, d_skip, norm_w, batch, chunks):
    t = xbc.shape[0]
    q = SSD_CHUNK
    nc = t // batch // (q * chunks)
    pad = lambda v: jnp.pad(v.astype(F32), (0, LANES - v.shape[0]))[None, :]
    expand = (np.arange(LANES)[:, None] == (np.arange(SSD_INNER)[None, :] // SSD_HEAD_DIM)).astype(np.float32)
    expand = np.tile(expand, (SPLIT_PIECES, 1))
    tri = np.tile(np.tril(np.ones((q, q), np.float32)), (1, SPLIT_PIECES))
    row = lambda w: pl.BlockSpec((q * chunks, w), lambda b, c: (b * nc + c, 0))
    return pl.pallas_call(
        functools.partial(_ssd_kernel, chunks=chunks),
        grid=(batch, nc),
        in_specs=[row(SSD_XBC), row(SSD_INNER), row(LANES), _resident((SSD_CONV, SSD_XBC)), _resident((1, SSD_XBC)),
                  _resident((1, LANES)), _resident((1, LANES)), _resident((1, SSD_INNER)), _resident((1, SSD_INNER)),
                  _resident((SPLIT_PIECES * LANES, SSD_INNER)), _resident((q, SPLIT_PIECES * q))],
        out_specs=row(SSD_INNER),
        out_shape=jax.ShapeDtypeStruct((t, SSD_INNER), BF16),
        scratch_shapes=[pltpu.VMEM((q + SUBLANES, SSD_XBC), F32),
                        pltpu.VMEM((SSD_GROUPS, SSD_STATE, SSD_GROUP_WIDTH), F32),
                        pltpu.VMEM((q, SSD_INNER), F32)],
        compiler_params=_params(("arbitrary", "arbitrary")),
        name="ssd_scan",
    )(xbc, z, small, conv_w, conv_b[None, :], pad(dt_bias), pad(a_log),
      jnp.repeat(d_skip.astype(F32), SSD_HEAD_DIM)[None, :], norm_w[None, :], jnp.asarray(expand, BF16),
      jnp.asarray(tri, BF16))


def _gla_kernel(q_ref, k_ref, v_ref, g_ref, r_ref, nw_ref, tri_ref, o_ref,
                state_ref, k_pad, c_pad, v_pad, term_buf, intra_buf, *, chunks):
    @pl.when(pl.program_id(1) == 0)
    def _():
        state_ref[...] = jnp.zeros_like(state_ref)

    for c in range(chunks):
        rows = pl.ds(c * GLA_CHUNK, GLA_CHUNK)
        _gla_chunk(q_ref.at[rows], k_ref.at[rows], v_ref.at[rows], g_ref.at[rows], r_ref.at[rows], nw_ref, tri_ref,
                   o_ref.at[rows], state_ref, k_pad, c_pad, v_pad, term_buf, intra_buf)


def _gla_chunk(q_ref, k_ref, v_ref, g_ref, r_ref, nw_ref, tri_ref, o_ref,
               state_ref, k_pad, c_pad, v_pad, term_buf, intra_buf):
    n = GLA_CHUNK
    sub = GLA_SUB
    nsub = n // sub
    heads = [(slice(h * GLA_HK, (h + 1) * GLA_HK), slice(h * GLA_HV, (h + 1) * GLA_HV)) for h in range(GLA_HEADS)]
    row = lax.broadcasted_iota(jnp.int32, (n, n), 0)
    col = lax.broadcasted_iota(jnp.int32, (n, n), 1)
    cums = [_dot(tri_ref[...], jnp.concatenate(_split3(g_ref[:, ks]), axis=0)) for ks, _ in heads]

    def sub_ref(cum, i):
        return cum[i * sub - 1:i * sub, :] if i > 0 else jnp.zeros((1, GLA_HK), F32)

    def factored_blocks(h, first, clamp):
        ks, _ = heads[h]
        cum = cums[h]
        qh = q_ref[:, ks].astype(F32)
        kh = k_ref[:, ks].astype(F32)
        blocks = [jnp.zeros((sub, n), F32)] * first
        for i in range(first, nsub):
            ref = sub_ref(cum, i)
            qt = qh[i * sub:(i + 1) * sub, :] * jnp.exp2(cum[i * sub:(i + 1) * sub, :] - ref)
            kt = kh * jnp.exp2(jnp.minimum(ref - cum, clamp))
            blocks.append(_dot_nt(qt.astype(BF16), kt.astype(BF16)))
        return jnp.concatenate(blocks, axis=0)

    worst = jnp.max(jnp.concatenate(
        [sub_ref(cum, i) - cum[(i + 1) * sub - 1:(i + 1) * sub, :] for cum in cums for i in range(nsub)], axis=0))
    safe = worst <= GLA_SAFE_LOG2

    @pl.when(safe)
    def _():
        for h, (_, vs) in enumerate(heads):
            a = jnp.where(row >= col, factored_blocks(h, 0, GLA_SAFE_LOG2), 0.0)
            intra_buf[:, vs] = _dot(a.astype(BF16), v_ref[:, vs])

    @pl.when(jnp.logical_not(safe))
    def _():
        k_pad[0:sub, :] = jnp.zeros((sub, GLA_HK), F32)
        c_pad[0:sub, :] = jnp.zeros((sub, GLA_HK), F32)
        v_pad[0:sub, :] = jnp.zeros((sub, GLA_HV), F32)
        ones = jnp.ones((GLA_HK, GLA_HV), BF16)
        for h, (ks, vs) in enumerate(heads):
            cum = cums[h]
            qh = q_ref[:, ks].astype(F32)
            vh_b = v_ref[:, vs]
            a_far = jnp.where((row - col) >= sub, factored_blocks(h, 1, 0.0), 0.0)
            o = _dot(a_far.astype(BF16), vh_b)
            k_pad[sub:sub + n, :] = k_ref[:, ks].astype(F32)
            c_pad[sub:sub + n, :] = cum
            v_pad[sub:sub + n, :] = vh_b.astype(F32)
            for d in range(sub):
                kd = k_pad[sub - d:sub - d + n, :]
                cd = c_pad[sub - d:sub - d + n, :]
                term_buf[d * n:(d + 1) * n, :] = (qh * kd * jnp.exp2(cum - cd)).astype(BF16)
            near = _dot(term_buf[...], ones)
            for d in range(sub):
                o = o + near[d * n:(d + 1) * n, :] * v_pad[sub - d:sub - d + n, :]
            intra_buf[:, vs] = o

    for h, (ks, vs) in enumerate(heads):
        cum = cums[h]
        last = cum[n - 1:n, :]
        qh = q_ref[:, ks].astype(F32)
        kh = k_ref[:, ks].astype(F32)
        vh_b = v_ref[:, vs]
        st = state_ref[h]
        o = intra_buf[:, vs] + _dot_nt((qh * jnp.exp2(cum)).astype(BF16), st.astype(BF16))
        state_ref[h] = st * jnp.exp2(last) + _dot_tn(vh_b, (kh * jnp.exp2(last - cum)).astype(BF16))
        rf = r_ref[:, vs].astype(F32)
        o_ref[:, vs] = (_rms(o, nw_ref[...]) * (rf * _sigmoid(rf))).astype(o_ref.dtype)


def _gla_scan(q, k, v, logg, r, norm_w, batch, chunks):
    t = q.shape[0]
    n = GLA_CHUNK
    nc = t // batch // (n * chunks)
    tri = np.tile(np.tril(np.ones((n, n), np.float32)), (1, SPLIT_PIECES))
    row = lambda w: pl.BlockSpec((n * chunks, w), lambda b, c: (b * nc + c, 0))
    return pl.pallas_call(
        functools.partial(_gla_kernel, chunks=chunks),
        grid=(batch, nc),
        in_specs=[row(GLA_KEY), row(GLA_KEY), row(GLA_VAL), row(GLA_KEY), row(GLA_VAL),
                  _resident((1, GLA_HV)), _resident((n, SPLIT_PIECES * n))],
        out_specs=row(GLA_VAL),
        out_shape=jax.ShapeDtypeStruct((t, GLA_VAL), BF16),
        scratch_shapes=[pltpu.VMEM((GLA_HEADS, GLA_HV, GLA_HK), F32),
                        pltpu.VMEM((n + GLA_SUB, GLA_HK), F32),
                        pltpu.VMEM((n + GLA_SUB, GLA_HK), F32),
                        pltpu.VMEM((n + GLA_SUB, GLA_HV), F32),
                        pltpu.VMEM((GLA_SUB * n, GLA_HK), BF16),
                        pltpu.VMEM((n, GLA_VAL), F32)],
        compiler_params=_params(("arbitrary", "arbitrary")),
        name="gla_scan",
    )(q, k, v, logg, r, norm_w[None, :], jnp.asarray(tri, BF16))


def _proj_residual_kernel(*refs, widths):
    x_ref = refs[0]
    a_refs = refs[1:1 + len(widths)]
    w_ref = refs[1 + len(widths)]
    o_ref = refs[2 + len(widths)]
    for c0 in range(0, D_MODEL, MATMUL_COLS):
        cs = slice(c0, c0 + MATMUL_COLS)
        acc = x_ref[:, cs]
        off = 0
        for a_ref, width in zip(a_refs, widths):
            acc = acc + _dot(a_ref[...], w_ref[off:off + width, cs])
            off += width
        o_ref[:, cs] = acc


def _proj_residual(x, acts, w, tile):
    t = x.shape[0]
    tm = min(tile, t)
    widths = tuple(a.shape[1] for a in acts)
    row = lambda wd: pl.BlockSpec((tm, wd), lambda i: (i, 0))
    return pl.pallas_call(
        functools.partial(_proj_residual_kernel, widths=widths),
        grid=(t // tm,),
        in_specs=[row(D_MODEL)] + [row(wd) for wd in widths] + [_resident(w.shape)],
        out_specs=row(D_MODEL),
        out_shape=jax.ShapeDtypeStruct((t, D_MODEL), F32),
        compiler_params=_params(("parallel",)),
        name="proj_residual",
    )(x, *acts, w)


def _norm_proj_kernel(x_ref, g_ref, w_ref, *o_refs):
    xn = _rms(x_ref[...], g_ref[...]).astype(BF16)
    off = 0
    for o_ref in o_refs:
        width = o_ref.shape[1]
        _proj_cols(xn, w_ref, off, width, o_ref)
        off += width


def _norm_proj(x, gain, w, widths):
    t = x.shape[0]
    tm = min(TOKEN_TILE, t)
    row = lambda wd: pl.BlockSpec((tm, wd), lambda i: (i, 0))
    return pl.pallas_call(
        _norm_proj_kernel,
        grid=(t // tm,),
        in_specs=[row(D_MODEL), _resident((1, D_MODEL)), _resident(w.shape)],
        out_specs=[row(wd) for wd in widths],
        out_shape=[jax.ShapeDtypeStruct((t, wd), BF16) for wd in widths],
        compiler_params=_params(("parallel",)),
        name="norm_proj",
    )(x, gain, w)


def _qkv_proj_kernel(x_ref, g_ref, wqt_ref, wk_ref, wvt_ref, qt_ref, k_ref, vt_ref):
    xn = _rms(x_ref[...], g_ref[...]).astype(BF16)
    for c0 in range(0, DIFF_QK, MATMUL_COLS):
        cs = slice(c0, c0 + MATMUL_COLS)
        qt_ref[cs, :] = (_dot_nt(wqt_ref[cs, :], xn) * DIFF_Q_SCALE).astype(BF16)
        vt_ref[cs, :] = _dot_nt(wvt_ref[cs, :], xn).astype(BF16)
    _proj_cols(xn, wk_ref, 0, DIFF_QK, k_ref)


def _qkv_proj(x, gain, wq_t, wk, wv_t, tm):
    t = x.shape[0]
    row = lambda wd: pl.BlockSpec((tm, wd), lambda i: (i, 0))
    tr = pl.BlockSpec((None, DIFF_QK, tm), lambda i: (i, 0, 0))
    tr_shape = jax.ShapeDtypeStruct((t // tm, DIFF_QK, tm), BF16)
    return pl.pallas_call(
        _qkv_proj_kernel,
        grid=(t // tm,),
        in_specs=[row(D_MODEL), _resident((1, D_MODEL)), _resident(wq_t.shape), _resident(wk.shape),
                  _resident(wv_t.shape)],
        out_specs=[tr, row(DIFF_QK), tr],
        out_shape=[tr_shape, jax.ShapeDtypeStruct((t, DIFF_QK), BF16), tr_shape],
        compiler_params=_params(("parallel",)),
        name="qkv_proj",
    )(x, gain, wq_t, wk, wv_t)


def _diff_attn_kernel(slope_ref, qt_ref, k_ref, kx_ref, vt_ref, lam_ref, nw_ref, o_ref,
                      acc_ref, m_ref, l_ref, qa_ref, s_a, s_b, mx_a, mx_b, *, lambda_init, fuse_max, mxu_denominator):
    tile = qt_ref.shape[1]
    h = pl.program_id(1)
    i = pl.program_id(2)
    tile_step = slope_ref[h] * (LOG2E * tile)
    half = DIFF_HEAD_DIM

    qt = qt_ref[...]
    zeros = jnp.zeros((half, tile), BF16)
    ones_rows = (lax.broadcasted_iota(jnp.int32, (LANES, tile), 0) < ALIBI_PIECES).astype(BF16)
    qa_ref[:, :tile] = jnp.concatenate([qt[:half], zeros, ones_rows], axis=0)
    qa_ref[:, tile:] = jnp.concatenate([zeros, qt[half:], ones_rows], axis=0)
    acc_ref[...] = jnp.zeros_like(acc_ref)
    l_ref[...] = jnp.zeros_like(l_ref)
    m_ref[...] = jnp.full_like(m_ref, NEG_BIG)

    def scores(t, buf, masked):
        s_ref, mx_ref = buf
        start = pl.multiple_of(t * tile, tile)
        k_aug = jnp.concatenate([k_ref[pl.ds(start, tile), :], kx_ref[...]], axis=1)
        s = _dot(k_aug, qa_ref[...])
        if masked:
            key_pos = lax.broadcasted_iota(jnp.int32, (tile, 2 * tile), 0)
            lane_pos = lax.broadcasted_iota(jnp.int32, (tile, 2 * tile), 1)
            causal = key_pos <= jnp.where(lane_pos >= tile, lane_pos - tile, lane_pos)
            s = jnp.where(causal, s, NEG_BIG)
        s_ref[...] = s
        if fuse_max:
            mx_ref[...] = jnp.max(s.reshape(tile // SUBLANES, SUBLANES, 2 * tile), axis=0)

    def softmax_pv(t, buf):
        s_ref, mx_ref = buf
        off = -tile_step * (i - t).astype(F32)
        s = s_ref[...]
        m_old = m_ref[0:1, :]
        col_max = jnp.max(mx_ref[...] if fuse_max else s, axis=0, keepdims=True)
        m_new = jnp.maximum(m_old, col_max + off)
        alpha = jnp.exp2(m_old - m_new)
        p = jnp.exp2(s + (off - m_new))
        if mxu_denominator:
            v_aug = jnp.concatenate([vt_ref[t], jnp.ones((DENOM_ROWS, tile), BF16)], axis=0)
            acc_ref[...] = alpha * acc_ref[...] + _dot(v_aug, p.astype(BF16))
        else:
            l_ref[0:1, :] = alpha * l_ref[0:1, :] + jnp.sum(p, axis=0, keepdims=True)
            acc_ref[0:DIFF_V_DIM, :] = alpha * acc_ref[0:DIFF_V_DIM, :] + _dot(vt_ref[t], p.astype(BF16))
        m_ref[0:1, :] = m_new

    buf_a = (s_a, mx_a)
    buf_b = (s_b, mx_b)
    even = i % 2 == 0

    @pl.when(i == 0)
    def _():
        scores(0, buf_a, True)

    @pl.when(jnp.logical_and(even, i > 0))
    def _():
        scores(0, buf_a, False)
        scores(1, buf_b, False)
        softmax_pv(0, buf_a)

    @pl.when(jnp.logical_not(even))
    def _():
        scores(0, buf_b, False)

    first = jnp.where(even, 1, 0)

    def pair(u, carry):
        t = first + 2 * u
        scores(t + 1, buf_a, False)
        softmax_pv(t, buf_b)
        scores(t + 2, buf_b, False)
        softmax_pv(t + 1, buf_a)
        return carry

    lax.fori_loop(0, (i - 1) // 2, pair, 0)

    @pl.when(i > 0)
    def _():
        scores(i, buf_a, True)
        softmax_pv(i - 1, buf_b)

    softmax_pv(i, buf_a)

    lam = (jnp.exp(jnp.sum(lam_ref[0:1, :] * lam_ref[1:2, :], axis=-1, keepdims=True))
           - jnp.exp(jnp.sum(lam_ref[2:3, :] * lam_ref[3:4, :], axis=-1, keepdims=True)) + lambda_init)
    denom = acc_ref[DIFF_V_DIM:DIFF_V_DIM + 1, :] if mxu_denominator else l_ref[0:1, :]
    normed = acc_ref[0:DIFF_V_DIM, :] / denom
    o = (normed[:, :tile] - lam * normed[:, tile:]).T
    o_ref[...] = (_rms(o, nw_ref[...]) * (1.0 - lambda_init)).astype(o_ref.dtype)


def _alibi_slopes(n):
    start = 2.0 ** (-8.0 / n)
    return np.array([start ** (i + 1) for i in range(n)], dtype=np.float32)


def _alibi_key_columns(tile):
    target = (LOG2E * _alibi_slopes(DIFF_HEADS))[:, None] * np.arange(tile, dtype=np.float32)[None, :]
    rest = jnp.asarray(target, F32)
    pieces = []
    for _ in range(ALIBI_PIECES):
        piece = rest.astype(BF16)
        pieces.append(piece)
        rest = rest - piece.astype(F32)
    cols = jnp.stack(pieces, axis=-1)
    return jnp.pad(cols, ((0, 0), (0, 0), (0, LANES - ALIBI_PIECES)))


def _diff_attention(qt, k, vt, lam_vecs, subln, lambda_init, batch, fuse_max, mxu_denominator):
    nblk, _, tile = qt.shape
    nq = nblk // batch
    seq = nq * tile
    return pl.pallas_call(
        functools.partial(_diff_attn_kernel, lambda_init=lambda_init, fuse_max=fuse_max,
                          mxu_denominator=mxu_denominator),
        grid=(batch, DIFF_HEADS, nq),
        in_specs=[pl.BlockSpec(memory_space=pltpu.SMEM),
                  pl.BlockSpec((None, LANES, tile), lambda b, h, i: (b * nq + i, h, 0)),
                  pl.BlockSpec((seq, LANES), lambda b, h, i: (b, h)),
                  pl.BlockSpec((None, tile, LANES), lambda b, h, i: (h, 0, 0)),
                  pl.BlockSpec((nq, LANES, tile), lambda b, h, i: (b, h, 0)),
                  _resident((4, DIFF_HEAD_DIM)), _resident((1, DIFF_V_DIM))],
        out_specs=pl.BlockSpec((tile, LANES), lambda b, h, i: (b * nq + i, h)),
        out_shape=jax.ShapeDtypeStruct((batch * seq, DIFF_HEADS * DIFF_V_DIM), BF16),
        scratch_shapes=[pltpu.VMEM((DIFF_V_DIM + DENOM_ROWS, 2 * tile), F32), pltpu.VMEM((SUBLANES, 2 * tile), F32),
                        pltpu.VMEM((SUBLANES, 2 * tile), F32), pltpu.VMEM((2 * LANES, 2 * tile), BF16),
                        pltpu.VMEM((tile, 2 * tile), F32), pltpu.VMEM((tile, 2 * tile), F32),
                        pltpu.VMEM((SUBLANES, 2 * tile), F32), pltpu.VMEM((SUBLANES, 2 * tile), F32)],
        compiler_params=_params(("arbitrary", "arbitrary", "arbitrary")),
        name="diff_attention",
    )(jnp.asarray(_alibi_slopes(DIFF_HEADS)), qt, k, _alibi_key_columns(tile), vt, lam_vecs, subln[None, :])


def _cross_attn_kernel(x_ref, g_ref, wq_ref, k_ref, v_ref, wo_ref, o_ref, q_buf, att_buf):
    heads = [slice(h * X_HEAD_DIM, (h + 1) * X_HEAD_DIM) for h in range(X_HEADS)]
    xn = _rms(x_ref[...], g_ref[...]).astype(BF16)
    for c0 in range(0, D_MODEL, MATMUL_COLS):
        cs = slice(c0, c0 + MATMUL_COLS)
        q_buf[:, cs] = _dot(xn, wq_ref[:, cs]).astype(BF16)
    scores = [_dot_nt(q_buf[:, hs], k_ref[:, hs]) * (X_HEAD_DIM ** -0.5) for hs in heads]
    for hs, s in zip(heads, scores):
        p = jnp.exp(s - jnp.max(s, axis=-1, keepdims=True))
        pv = _dot(p.astype(BF16), v_ref[:, hs])
        att_buf[:, hs] = (pv / jnp.sum(p, axis=-1, keepdims=True)).astype(BF16)
    for c0 in range(0, D_MODEL, MATMUL_COLS):
        cs = slice(c0, c0 + MATMUL_COLS)
        o_ref[:, cs] = x_ref[:, cs] + _dot(att_buf[...], wo_ref[:, cs])


def _cross_attention(x, gain, wq, k_mem, v_mem, wo, batch):
    t = x.shape[0]
    seq = t // batch
    tm = min(CROSS_TILE, seq)
    per = seq // tm
    n_mem = k_mem.shape[0] // batch
    row = pl.BlockSpec((tm, D_MODEL), lambda i: (i, 0))
    mem = pl.BlockSpec((n_mem, D_MODEL), lambda i: (i // per, 0))
    return pl.pallas_call(
        _cross_attn_kernel,
        grid=(t // tm,),
        in_specs=[row, _resident((1, D_MODEL)), _resident(wq.shape), mem, mem, _resident(wo.shape)],
        out_specs=row,
        out_shape=jax.ShapeDtypeStruct((t, D_MODEL), F32),
        scratch_shapes=[pltpu.VMEM((tm, D_MODEL), BF16), pltpu.VMEM((tm, D_MODEL), BF16)],
        compiler_params=_params(("parallel",)),
        name="cross_attention",
    )(x, gain, wq, k_mem, v_mem, wo)


def _mlp_kernel(x_ref, g_ref, w1_ref, w2_ref, fg_ref, o_ref, acc_ref, *, final_norm):
    x = x_ref[...]
    xn = _rms(x, g_ref[...]).astype(BF16)
    acc_ref[...] = x
    for c0 in range(0, D_FF, MATMUL_COLS):
        hid = jnp.maximum(_dot(xn, w1_ref[:, c0:c0 + MATMUL_COLS]), 0.0)
        acc_ref[...] += _dot((hid * hid).astype(BF16), w2_ref[c0:c0 + MATMUL_COLS, :])
    out = acc_ref[...]
    if final_norm:
        out = _rms(out, fg_ref[...])
    o_ref[...] = out


def _mlp(x, gain, w1, w2, final_gain, final_norm, tile):
    t = x.shape[0]
    tm = min(tile, t)
    row = pl.BlockSpec((tm, D_MODEL), lambda i: (i, 0))
    return pl.pallas_call(
        functools.partial(_mlp_kernel, final_norm=final_norm),
        grid=(t // tm,),
        in_specs=[row, _resident((1, D_MODEL)), _resident(w1.shape), _resident(w2.shape), _resident((1, D_MODEL))],
        out_specs=row,
        out_shape=jax.ShapeDtypeStruct((t, D_MODEL), F32),
        scratch_shapes=[pltpu.VMEM((tm, D_MODEL), F32)],
        compiler_params=_params(("parallel",)),
        name="mlp",
    )(x, gain, w1, w2, final_gain)


def _even_weights(w_in, gla_w2):
    z_end = SSD_INNER
    xbc_end = z_end + SSD_XBC
    dt_end = xbc_end + SSD_HEADS
    v_end = dt_end + 2 * GLA_KEY + GLA_VAL
    glr_end = v_end + GLA_RANK
    w_main = jnp.concatenate([w_in[:, :xbc_end], w_in[:, dt_end:v_end], w_in[:, glr_end:]], axis=1).astype(BF16)
    w_small = jnp.concatenate([w_in[:, xbc_end:dt_end], w_in[:, v_end:glr_end],
                               jnp.zeros((D_MODEL, LANES - SSD_HEADS - GLA_RANK), w_in.dtype)], axis=1).astype(BF16)
    w2_pad = jnp.zeros((LANES, GLA_KEY), F32).at[SSD_HEADS:SSD_HEADS + GLA_RANK].set(gla_w2).astype(BF16)
    return w_main, w_small, w2_pad


def kernel(x, mem, ev_norm, ev_w_in, ev_conv_w, ev_conv_b, ev_dt_bias, ev_a_log, ev_d_skip, ev_ssd_norm, ev_gla_w2, ev_gla_b, ev_gla_norm, ev_w_out, od_norm, od_w_qkv, od_lam_q1, od_lam_k1, od_lam_q2, od_lam_k2, od_subln, od_w_o, xa_norm, xa_mem_norm, xa_wq, xa_wkv, xa_wo, mlp_norm, mlp_w1, mlp_w2, final_norm):
    batch, seq, d = x.shape
    n_mem = mem.shape[1]
    n_layers = xa_norm.shape[0]
    xf = x.reshape(batch * seq, d)
    memf = mem.reshape(batch * n_mem, d)
    for layer in range(n_layers):
        i = layer // 2
        tail_tile = 2 * TOKEN_TILE
        if layer % 2 == 0:
            w_main, w_small, w2_pad = _even_weights(ev_w_in[i], ev_gla_w2[i])
            z, xbc, q, k, v, r, small, logg = _even_in_proj(xf, ev_norm[i][None, :], w_main, w_small, w2_pad,
                                                            ev_gla_b[i][None, :])
            y = _ssd_scan(xbc, z, small, ev_conv_w[i], ev_conv_b[i], ev_dt_bias[i], ev_a_log[i], ev_d_skip[i],
                          ev_ssd_norm[i], batch, 1)
            o = _gla_scan(q, k, v, logg, r, ev_gla_norm[i], batch, 4 if i == 0 else 8)
            xf = _proj_residual(xf, [y, o], ev_w_out[i].astype(BF16), tail_tile)
        else:
            lambda_init = 0.8 - 0.6 * math.exp(-0.3 * layer)
            w_qkv = od_w_qkv[i].astype(BF16)
            qt, k, vt = _qkv_proj(xf, od_norm[i][None, :], w_qkv[:, :DIFF_QK].T, w_qkv[:, DIFF_QK:2 * DIFF_QK],
                                  w_qkv[:, 2 * DIFF_QK:].T, min(ATTN_TILE, seq))
            lam_vecs = jnp.stack([od_lam_q1[i], od_lam_k1[i], od_lam_q2[i], od_lam_k2[i]]).astype(F32)
            att = _diff_attention(qt, k, vt, lam_vecs, od_subln[i], lambda_init, batch, True, i == 1)
            xf = _proj_residual(xf, [att], od_w_o[i].astype(BF16), tail_tile)
        k_mem, v_mem = _norm_proj(memf, xa_mem_norm[layer][None, :], xa_wkv[layer].astype(BF16), (d, d))
        xf = _cross_attention(xf, xa_norm[layer][None, :], xa_wq[layer].astype(BF16), k_mem, v_mem,
                              xa_wo[layer].astype(BF16), batch)
        xf = _mlp(xf, mlp_norm[layer][None, :], mlp_w1[layer].astype(BF16), mlp_w2[layer].astype(BF16),
                  final_norm[None, :], layer == n_layers - 1, tail_tile)
    return xf.reshape(batch, seq, d)
```

```python
import functools
import math

import numpy as np
import jax
import jax.numpy as jnp
from jax import lax
from jax.experimental import pallas as pl
from jax.experimental.pallas import tpu as pltpu

F32 = jnp.float32
BF16 = jnp.bfloat16
EPS = 1e-5
NEG_BIG = -1e30

D_MODEL = 1024
N_LAYERS = 4

SSD_HEADS = 16
SSD_HEAD_DIM = 64
SSD_INNER = SSD_HEADS * SSD_HEAD_DIM
SSD_GROUPS = 2
SSD_STATE = 64
SSD_CONV = 4
SSD_CHUNK = 128
SSD_XBC = SSD_INNER + 2 * SSD_GROUPS * SSD_STATE
SSD_GROUP_WIDTH = SSD_INNER // SSD_GROUPS
SSD_GROUP_HEADS = SSD_HEADS // SSD_GROUPS

GLA_HEADS = 4
GLA_KEY = 512
GLA_VAL = 1024
GLA_HK = GLA_KEY // GLA_HEADS
GLA_HV = GLA_VAL // GLA_HEADS
GLA_RANK = 16
GLA_TAU = 16.0
GLA_CHUNK = 64
GLA_SUB = 16
GLA_SAFE_LOG2 = 100.0

DIFF_HEADS = 8
DIFF_HEAD_DIM = 64
DIFF_V_DIM = 2 * DIFF_HEAD_DIM
DIFF_QK = DIFF_HEADS * 2 * DIFF_HEAD_DIM
LOG2E = 1.4426950408889634
DIFF_Q_SCALE = DIFF_HEAD_DIM ** -0.5 * LOG2E
ALIBI_PIECES = 3
DENOM_ROWS = 16
SPLIT_PIECES = 3

X_HEADS = 4
X_HEAD_DIM = D_MODEL // X_HEADS
D_FF = 4 * D_MODEL

LANES = 128
SUBLANES = 8
VMEM_LIMIT = 56 * 1024 * 1024
TOKEN_TILE = 512
ATTN_TILE = 1024
CROSS_TILE = 1024
MATMUL_COLS = 512


def _resident(shape):
    nd = len(shape)
    return pl.BlockSpec(shape, lambda *_: (0,) * nd, pipeline_mode=pl.Buffered(1))


def _params(semantics):
    return pltpu.CompilerParams(dimension_semantics=semantics, vmem_limit_bytes=VMEM_LIMIT)


def _rms(x, gain):
    return x * lax.rsqrt(jnp.mean(x * x, axis=-1, keepdims=True) + EPS) * gain


def _sigmoid(x):
    return 1.0 / (1.0 + jnp.exp(-x))


def _softplus(x):
    return jnp.maximum(x, 0.0) + jnp.log1p(jnp.exp(-jnp.abs(x)))


def _dot(a, b):
    return jnp.dot(a, b, preferred_element_type=F32)


def _dot_nt(a, b):
    return lax.dot_general(a, b, (((1,), (1,)), ((), ())), preferred_element_type=F32)


def _dot_tn(a, b):
    return lax.dot_general(a, b, (((0,), (0,)), ((), ())), preferred_element_type=F32)


def _split3(x):
    pieces = []
    rest = x
    for _ in range(SPLIT_PIECES):
        piece = rest.astype(BF16)
        pieces.append(piece)
        rest = rest - piece.astype(F32)
    return pieces


def _proj_cols(xn, w_ref, off, width, o_ref, scale=None):
    for c0 in range(0, width, MATMUL_COLS):
        cw = min(MATMUL_COLS, width - c0)
        acc = _dot(xn, w_ref[:, off + c0:off + c0 + cw])
        if scale is not None:
            acc = acc * scale
        o_ref[:, c0:c0 + cw] = acc.astype(o_ref.dtype)


_EVEN_MAIN = (SSD_INNER, SSD_XBC, GLA_KEY, GLA_KEY, GLA_VAL, GLA_VAL)


def _even_in_kernel(x_ref, g_ref, w_ref, ws_ref, w2_ref, gb_ref,
                    z_ref, xbc_ref, q_ref, k_ref, v_ref, r_ref, small_ref, logg_ref):
    xn = _rms(x_ref[...], g_ref[...]).astype(BF16)
    off = 0
    for width, o_ref, scale in zip(_EVEN_MAIN, (z_ref, xbc_ref, q_ref, k_ref, v_ref, r_ref),
                                   (None, None, GLA_HK ** -0.5, None, None, None)):
        _proj_cols(xn, w_ref, off, width, o_ref, scale)
        off += width
    small = _dot(xn, ws_ref[...])
    small_ref[...] = small
    logits = _dot(small.astype(BF16), w2_ref[...]) + gb_ref[...]
    logg_ref[...] = -_softplus(-logits) * (LOG2E / GLA_TAU)


def _even_in_proj(x, gain, w_main, w_small, w2_pad, gla_b):
    t = x.shape[0]
    tm = min(TOKEN_TILE, t)
    row = lambda w: pl.BlockSpec((tm, w), lambda i: (i, 0))
    out_shape = [jax.ShapeDtypeStruct((t, w), BF16) for w in _EVEN_MAIN]
    out_shape += [jax.ShapeDtypeStruct((t, LANES), F32), jax.ShapeDtypeStruct((t, GLA_KEY), F32)]
    return pl.pallas_call(
        _even_in_kernel,
        grid=(t // tm,),
        in_specs=[row(D_MODEL), _resident((1, D_MODEL)), _resident(w_main.shape), _resident(w_small.shape),
                  _resident(w2_pad.shape), _resident((1, GLA_KEY))],
        out_specs=[row(w) for w in _EVEN_MAIN] + [row(LANES), row(GLA_KEY)],
        out_shape=out_shape,
        compiler_params=_params(("parallel",)),
        name="even_in_proj",
    )(x, gain, w_main, w_small, w2_pad, gla_b)


def _ssd_kernel(xbc_ref, z_ref, small_ref, cw_ref, cb_ref, dtb_ref, alog_ref, dskip_ref, nw_ref, expand_ref,
                tri_ref, y_ref, conv_buf, state_ref, y_buf, *, chunks):
    @pl.when(pl.program_id(1) == 0)
    def _():
        conv_buf[0:SUBLANES, :] = jnp.zeros((SUBLANES, SSD_XBC), F32)
        state_ref[...] = jnp.zeros_like(state_ref)

    for c in range(chunks):
        rows = pl.ds(c * SSD_CHUNK, SSD_CHUNK)
        _ssd_chunk(xbc_ref.at[rows], z_ref.at[rows], small_ref.at[rows], cw_ref, cb_ref, dtb_ref, alog_ref,
                   dskip_ref, nw_ref, expand_ref, tri_ref, y_ref.at[rows], conv_buf, state_ref, y_buf)


def _ssd_chunk(xbc_ref, z_ref, small_ref, cw_ref, cb_ref, dtb_ref, alog_ref, dskip_ref, nw_ref, expand_ref,
               tri_ref, y_ref, conv_buf, state_ref, y_buf):
    q = SSD_CHUNK
    conv_buf[SUBLANES:SUBLANES + q, :] = xbc_ref[...].astype(F32)
    acc = cb_ref[...]
    for tap in range(SSD_CONV):
        start = SUBLANES - (SSD_CONV - 1) + tap
        acc = acc + cw_ref[tap:tap + 1, :] * conv_buf[start:start + q, :]
    conv_buf[0:SUBLANES, :] = conv_buf[q:q + SUBLANES, :]
    u = acc * _sigmoid(acc)
    xs = u[:, :SSD_INNER]
    bm = u[:, SSD_INNER:SSD_INNER + SSD_GROUPS * SSD_STATE]
    cm = u[:, SSD_INNER + SSD_GROUPS * SSD_STATE:]
    bm_t = bm.T

    lane = lax.broadcasted_iota(jnp.int32, (q, LANES), 1)
    dt = jnp.where(lane < SSD_HEADS, _softplus(small_ref[...] + dtb_ref[...]), 0.0)
    loga = dt * (-jnp.exp(alog_ref[...]) * LOG2E)
    cum = _dot(tri_ref[...], jnp.concatenate(_split3(loga), axis=0))
    cum_row = cum.T
    expand = expand_ref[...]
    dt_full = _dot(jnp.concatenate(_split3(dt), axis=1), expand)
    cum_full = _dot(jnp.concatenate(_split3(cum), axis=1), expand)
    last_full = cum_full[q - 1:q, :]
    xdt = xs * dt_full
    xdt_b = xdt.astype(BF16)
    tril = lax.broadcasted_iota(jnp.int32, (q, q), 0) >= lax.broadcasted_iota(jnp.int32, (q, q), 1)

    for g in range(SSD_GROUPS):
        gs = slice(g * SSD_GROUP_WIDTH, (g + 1) * SSD_GROUP_WIDTH)
        cg = cm[:, g * SSD_STATE:(g + 1) * SSD_STATE].astype(BF16)
        bg = bm[:, g * SSD_STATE:(g + 1) * SSD_STATE].astype(BF16)
        scores = _dot_nt(cg, bg)
        st = state_ref[g]
        y_inter = _dot(cg, st.astype(BF16)) * jnp.exp2(cum_full[:, gs])
        for e in range(SSD_GROUP_HEADS):
            h = g * SSD_GROUP_HEADS + e
            hs = slice(h * SSD_HEAD_DIM, (h + 1) * SSD_HEAD_DIM)
            seg = cum[:, h:h + 1] - cum_row[h:h + 1, :]
            decay = jnp.exp2(jnp.where(tril, seg, NEG_BIG))
            ye = _dot((scores * decay).astype(BF16), xdt_b[:, hs])
            y_buf[:, hs] = ye + y_inter[:, e * SSD_HEAD_DIM:(e + 1) * SSD_HEAD_DIM]
        wts = jnp.exp2(last_full[:, gs] - cum_full[:, gs])
        xw = (xdt[:, gs] * wts).astype(BF16)
        bg_t = bm_t[g * SSD_STATE:(g + 1) * SSD_STATE, :].astype(BF16)
        state_ref[g] = st * jnp.exp2(last_full[:, gs]) + _dot(bg_t, xw)

    zf = z_ref[...].astype(F32)
    y = (y_buf[...] + dskip_ref[...] * xs) * (zf * _sigmoid(zf))
    for g in range(SSD_GROUPS):
        gs = slice(g * SSD_GROUP_WIDTH, (g + 1) * SSD_GROUP_WIDTH)
        y_ref[:, gs] = _rms(y[:, gs], nw_ref[:, gs]).astype(y_ref.dtype)


def _ssd_scan(xbc, z, small, conv_w, conv_b, dt_bias, a_log, d_skip, norm_w, batch, chunks):
    t = xbc.shape[0]
    q = SSD_CHUNK
    nc = t // batch // (q * chunks)
    pad = lambda v: jnp.pad(v.astype(F32), (0, LANES - v.shape[0]))[None, :]
    expand = (np.arange(LANES)[:, None] == (np.arange(SSD_INNER)[None, :] // SSD_HEAD_DIM)).astype(np.float32)
    expand = np.tile(expand, (SPLIT_PIECES, 1))
    tri = np.tile(np.tril(np.ones((q, q), np.float32)), (1, SPLIT_PIECES))
    row = lambda w: pl.BlockSpec((q * chunks, w), lambda b, c: (b * nc + c, 0))
    return pl.pallas_call(
        functools.partial(_ssd_kernel, chunks=chunks),
        grid=(batch, nc),
        in_specs=[row(SSD_XBC), row(SSD_INNER), row(LANES), _resident((SSD_CONV, SSD_XBC)), _resident((1, SSD_XBC)),
                  _resident((1, LANES)), _resident((1, LANES)), _resident((1, SSD_INNER)), _resident((1, SSD_INNER)),
                  _resident((SPLIT_PIECES * LANES, SSD_INNER)), _resident((q, SPLIT_PIECES * q))],
        out_specs=row(SSD_INNER),
        out_shape=jax.ShapeDtypeStruct((t, SSD_INNER), BF16),
        scratch_shapes=[pltpu.VMEM((q + SUBLANES, SSD_XBC), F32),
                        pltpu.VMEM((SSD_GROUPS, SSD_STATE, SSD_GROUP_WIDTH), F32),
                        pltpu.VMEM((q, SSD_INNER), F32)],
        compiler_params=_params(("arbitrary", "arbitrary")),
        name="ssd_scan",
    )(xbc, z, small, conv_w, conv_b[None, :], pad(dt_bias), pad(a_log),
      jnp.repeat(d_skip.astype(F32), SSD_HEAD_DIM)[None, :], norm_w[None, :], jnp.asarray(expand, BF16),
      jnp.asarray(tri, BF16))


def _gla_kernel(q_ref, k_ref, v_ref, g_ref, r_ref, nw_ref, tri_ref, o_ref,
                state_ref, k_pad, c_pad, v_pad, term_buf, intra_buf, *, chunks):
    @pl.when(pl.program_id(1) == 0)
    def _():
        state_ref[...] = jnp.zeros_like(state_ref)

    for c in range(chunks):
        rows = pl.ds(c * GLA_CHUNK, GLA_CHUNK)
        _gla_chunk(q_ref.at[rows], k_ref.at[rows], v_ref.at[rows], g_ref.at[rows], r_ref.at[rows], nw_ref, tri_ref,
                   o_ref.at[rows], state_ref, k_pad, c_pad, v_pad, term_buf, intra_buf)


def _gla_chunk(q_ref, k_ref, v_ref, g_ref, r_ref, nw_ref, tri_ref, o_ref,
               state_ref, k_pad, c_pad, v_pad, term_buf, intra_buf):
    n = GLA_CHUNK
    sub = GLA_SUB
    nsub = n // sub
    heads = [(slice(h * GLA_HK, (h + 1) * GLA_HK), slice(h * GLA_HV, (h + 1) * GLA_HV)) for h in range(GLA_HEADS)]
    row = lax.broadcasted_iota(jnp.int32, (n, n), 0)
    col = lax.broadcasted_iota(jnp.int32, (n, n), 1)
    cums = [_dot(tri_ref[...], jnp.concatenate(_split3(g_ref[:, ks]), axis=0)) for ks, _ in heads]

    def sub_ref(cum, i):
        return cum[i * sub - 1:i * sub, :] if i > 0 else jnp.zeros((1, GLA_HK), F32)

    def factored_blocks(h, first, clamp):
        ks, _ = heads[h]
        cum = cums[h]
        qh = q_ref[:, ks].astype(F32)
        kh = k_ref[:, ks].astype(F32)
        blocks = [jnp.zeros((sub, n), F32)] * first
        for i in range(first, nsub):
            ref = sub_ref(cum, i)
            qt = qh[i * sub:(i + 1) * sub, :] * jnp.exp2(cum[i * sub:(i + 1) * sub, :] - ref)
            kt = kh * jnp.exp2(jnp.minimum(ref - cum, clamp))
            blocks.append(_dot_nt(qt.astype(BF16), kt.astype(BF16)))
        return jnp.concatenate(blocks, axis=0)

    worst = jnp.max(jnp.concatenate(
        [sub_ref(cum, i) - cum[(i + 1) * sub - 1:(i + 1) * sub, :] for cum in cums for i in range(nsub)], axis=0))
    safe = worst <= GLA_SAFE_LOG2

    @pl.when(safe)
    def _():
        for h, (_, vs) in enumerate(heads):
            a = jnp.where(row >= col, factored_blocks(h, 0, GLA_SAFE_LOG2), 0.0)
            intra_buf[:, vs] = _dot(a.astype(BF16), v_ref[:, vs])

    @pl.when(jnp.logical_not(safe))
    def _():
        k_pad[0:sub, :] = jnp.zeros((sub, GLA_HK), F32)
        c_pad[0:sub, :] = jnp.zeros((sub, GLA_HK), F32)
        v_pad[0:sub, :] = jnp.zeros((sub, GLA_HV), F32)
        ones = jnp.ones((GLA_HK, GLA_HV), BF16)
        for h, (ks, vs) in enumerate(heads):
            cum = cums[h]
            qh = q_ref[:, ks].astype(F32)
            vh_b = v_ref[:, vs]
            a_far = jnp.where((row - col) >= sub, factored_blocks(h, 1, 0.0), 0.0)
            o = _dot(a_far.astype(BF16), vh_b)
            k_pad[sub:sub + n, :] = k_ref[:, ks].astype(F32)
            c_pad[sub:sub + n, :] = cum
            v_pad[sub:sub + n, :] = vh_b.astype(F32)
            for d in range(sub):
                kd = k_pad[sub - d:sub - d + n, :]
                cd = c_pad[sub - d:sub - d + n, :]
                term_buf[d * n:(d + 1) * n, :] = (qh * kd * jnp.exp2(cum - cd)).astype(BF16)
            near = _dot(term_buf[...], ones)
            for d in range(sub):
                o = o + near[d * n:(d + 1) * n, :] * v_pad[sub - d:sub - d + n, :]
            intra_buf[:, vs] = o

    for h, (ks, vs) in enumerate(heads):
        cum = cums[h]
        last = cum[n - 1:n, :]
        qh = q_ref[:, ks].astype(F32)
        kh = k_ref[:, ks].astype(F32)
        vh_b = v_ref[:, vs]
        st = state_ref[h]
        o = intra_buf[:, vs] + _dot_nt((qh * jnp.exp2(cum)).astype(BF16), st.astype(BF16))
        state_ref[h] = st * jnp.exp2(last) + _dot_tn(vh_b, (kh * jnp.exp2(last - cum)).astype(BF16))
        rf = r_ref[:, vs].astype(F32)
        o_ref[:, vs] = (_rms(o, nw_ref[...]) * (rf * _sigmoid(rf))).astype(o_ref.dtype)


def _gla_scan(q, k, v, logg, r, norm_w, batch, chunks):
    t = q.shape[0]
    n = GLA_CHUNK
    nc = t // batch // (n * chunks)
    tri = np.tile(np.tril(np.ones((n, n), np.float32)), (1, SPLIT_PIECES))
    row = lambda w: pl.BlockSpec((n * chunks, w), lambda b, c: (b * nc + c, 0))
    return pl.pallas_call(
        functools.partial(_gla_kernel, chunks=chunks),
        grid=(batch, nc),
        in_specs=[row(GLA_KEY), row(GLA_KEY), row(GLA_VAL), row(GLA_KEY), row(GLA_VAL),
                  _resident((1, GLA_HV)), _resident((n, SPLIT_PIECES * n))],
        out_specs=row(GLA_VAL),
        out_shape=jax.ShapeDtypeStruct((t, GLA_VAL), BF16),
        scratch_shapes=[pltpu.VMEM((GLA_HEADS, GLA_HV, GLA_HK), F32),
                        pltpu.VMEM((n + GLA_SUB, GLA_HK), F32),
                        pltpu.VMEM((n + GLA_SUB, GLA_HK), F32),
                        pltpu.VMEM((n + GLA_SUB, GLA_HV), F32),
                        pltpu.VMEM((GLA_SUB * n, GLA_HK), BF16),
                        pltpu.VMEM((n, GLA_VAL), F32)],
        compiler_params=_params(("arbitrary", "arbitrary")),
        name="gla_scan",
    )(q, k, v, logg, r, norm_w[None, :], jnp.asarray(tri, BF16))


def _proj_residual_kernel(*refs, widths):
    x_ref = refs[0]
    a_refs = refs[1:1 + len(widths)]
    w_ref = refs[1 + len(widths)]
    o_ref = refs[2 + len(widths)]
    for c0 in range(0, D_MODEL, MATMUL_COLS):
        cs = slice(c0, c0 + MATMUL_COLS)
        acc = x_ref[:, cs]
        off = 0
        for a_ref, width in zip(a_refs, widths):
            acc = acc + _dot(a_ref[...], w_ref[off:off + width, cs])
            off += width
        o_ref[:, cs] = acc


def _proj_residual(x, acts, w, tile):
    t = x.shape[0]
    tm = min(tile, t)
    widths = tuple(a.shape[1] for a in acts)
    row = lambda wd: pl.BlockSpec((tm, wd), lambda i: (i, 0))
    return pl.pallas_call(
        functools.partial(_proj_residual_kernel, widths=widths),
        grid=(t // tm,),
        in_specs=[row(D_MODEL)] + [row(wd) for wd in widths] + [_resident(w.shape)],
        out_specs=row(D_MODEL),
        out_shape=jax.ShapeDtypeStruct((t, D_MODEL), F32),
        compiler_params=_params(("parallel",)),
        name="proj_residual",
    )(x, *acts, w)


def _norm_proj_kernel(x_ref, g_ref, w_ref, *o_refs):
    xn = _rms(x_ref[...], g_ref[...]).astype(BF16)
    off = 0
    for o_ref in o_refs:
        width = o_ref.shape[1]
        _proj_cols(xn, w_ref, off, width, o_ref)
        off += width


def _norm_proj(x, gain, w, widths):
    t = x.shape[0]
    tm = min(TOKEN_TILE, t)
    row = lambda wd: pl.BlockSpec((tm, wd), lambda i: (i, 0))
    return pl.pallas_call(
        _norm_proj_kernel,
        grid=(t // tm,),
        in_specs=[row(D_MODEL), _resident((1, D_MODEL)), _resident(w.shape)],
        out_specs=[row(wd) for wd in widths],
        out_shape=[jax.ShapeDtypeStruct((t, wd), BF16) for wd in widths],
        compiler_params=_params(("parallel",)),
        name="norm_proj",
    )(x, gain, w)


def _qkv_proj_kernel(x_ref, g_ref, wqt_ref, wk_ref, wvt_ref, qt_ref, k_ref, vt_ref):
    xn = _rms(x_ref[...], g_ref[...]).astype(BF16)
    for c0 in range(0, DIFF_QK, MATMUL_COLS):
        cs = slice(c0, c0 + MATMUL_COLS)
        qt_ref[cs, :] = (_dot_nt(wqt_ref[cs, :], xn) * DIFF_Q_SCALE).astype(BF16)
        vt_ref[cs, :] = _dot_nt(wvt_ref[cs, :], xn).astype(BF16)
    _proj_cols(xn, wk_ref, 0, DIFF_QK, k_ref)


def _qkv_proj(x, gain, wq_t, wk, wv_t, tm):
    t = x.shape[0]
    row = lambda wd: pl.BlockSpec((tm, wd), lambda i: (i, 0))
    tr = pl.BlockSpec((None, DIFF_QK, tm), lambda i: (i, 0, 0))
    tr_shape = jax.ShapeDtypeStruct((t // tm, DIFF_QK, tm), BF16)
    return pl.pallas_call(
        _qkv_proj_kernel,
        grid=(t // tm,),
        in_specs=[row(D_MODEL), _resident((1, D_MODEL)), _resident(wq_t.shape), _resident(wk.shape),
                  _resident(wv_t.shape)],
        out_specs=[tr, row(DIFF_QK), tr],
        out_shape=[tr_shape, jax.ShapeDtypeStruct((t, DIFF_QK), BF16), tr_shape],
        compiler_params=_params(("parallel",)),
        name="qkv_proj",
    )(x, gain, wq_t, wk, wv_t)


def _diff_attn_kernel(slope_ref, qt_ref, k_ref, kx_ref, vt_ref, lam_ref, nw_ref, o_ref,
                      acc_ref, m_ref, l_ref, qa_ref, s_a, s_b, mx_a, mx_b, *, lambda_init, fuse_max, mxu_denominator, bf16_exp):
    tile = qt_ref.shape[1]
    h = pl.program_id(1)
    i = pl.program_id(2)
    tile_step = slope_ref[h] * (LOG2E * tile)
    half = DIFF_HEAD_DIM

    qt = qt_ref[...]
    zeros = jnp.zeros((half, tile), BF16)
    ones_rows = (lax.broadcasted_iota(jnp.int32, (LANES, tile), 0) < ALIBI_PIECES).astype(BF16)
    qa_ref[:, :tile] = jnp.concatenate([qt[:half], zeros, ones_rows], axis=0)
    qa_ref[:, tile:] = jnp.concatenate([zeros, qt[half:], ones_rows], axis=0)
    acc_ref[...] = jnp.zeros_like(acc_ref)
    l_ref[...] = jnp.zeros_like(l_ref)
    m_ref[...] = jnp.full_like(m_ref, NEG_BIG)

    def scores(t, buf, masked):
        s_ref, mx_ref = buf
        start = pl.multiple_of(t * tile, tile)
        k_aug = jnp.concatenate([k_ref[pl.ds(start, tile), :], kx_ref[...]], axis=1)
        s = _dot(k_aug, qa_ref[...])
        if masked:
            key_pos = lax.broadcasted_iota(jnp.int32, (tile, 2 * tile), 0)
            lane_pos = lax.broadcasted_iota(jnp.int32, (tile, 2 * tile), 1)
            causal = key_pos <= jnp.where(lane_pos >= tile, lane_pos - tile, lane_pos)
            s = jnp.where(causal, s, NEG_BIG)
        s_ref[...] = s
        if fuse_max:
            mx_ref[...] = jnp.max(s.reshape(tile // SUBLANES, SUBLANES, 2 * tile), axis=0)

    def softmax_pv(t, buf):
        s_ref, mx_ref = buf
        off = -tile_step * (i - t).astype(F32)
        s = s_ref[...]
        m_old = m_ref[0:1, :]
        col_max = jnp.max(mx_ref[...] if fuse_max else s, axis=0, keepdims=True)
        m_new = jnp.maximum(m_old, col_max + off)
        alpha = jnp.exp2(m_old - m_new)
        if bf16_exp:
            p = jnp.exp2((s + (off - m_new)).astype(BF16))
        else:
            p = jnp.exp2(s + (off - m_new))
        if mxu_denominator:
            v_aug = jnp.concatenate([vt_ref[t], jnp.ones((DENOM_ROWS, tile), BF16)], axis=0)
            acc_ref[...] = alpha * acc_ref[...] + _dot(v_aug, p.astype(BF16))
        else:
            l_ref[0:1, :] = alpha * l_ref[0:1, :] + jnp.sum(p, axis=0, keepdims=True)
            acc_ref[0:DIFF_V_DIM, :] = alpha * acc_ref[0:DIFF_V_DIM, :] + _dot(vt_ref[t], p.astype(BF16))
        m_ref[0:1, :] = m_new

    buf_a = (s_a, mx_a)
    buf_b = (s_b, mx_b)
    even = i % 2 == 0

    @pl.when(i == 0)
    def _():
        scores(0, buf_a, True)

    @pl.when(jnp.logical_and(even, i > 0))
    def _():
        scores(0, buf_a, False)
        scores(1, buf_b, False)
        softmax_pv(0, buf_a)

    @pl.when(jnp.logical_not(even))
    def _():
        scores(0, buf_b, False)

    first = jnp.where(even, 1, 0)

    def pair(u, carry):
        t = first + 2 * u
        scores(t + 1, buf_a, False)
        softmax_pv(t, buf_b)
        scores(t + 2, buf_b, False)
        softmax_pv(t + 1, buf_a)
        return carry

    lax.fori_loop(0, (i - 1) // 2, pair, 0)

    @pl.when(i > 0)
    def _():
        scores(i, buf_a, True)
        softmax_pv(i - 1, buf_b)

    softmax_pv(i, buf_a)

    lam = (jnp.exp(jnp.sum(lam_ref[0:1, :] * lam_ref[1:2, :], axis=-1, keepdims=True))
           - jnp.exp(jnp.sum(lam_ref[2:3, :] * lam_ref[3:4, :], axis=-1, keepdims=True)) + lambda_init)
    denom = acc_ref[DIFF_V_DIM:DIFF_V_DIM + 1, :] if mxu_denominator else l_ref[0:1, :]
    normed = acc_ref[0:DIFF_V_DIM, :] / denom
    o = (normed[:, :tile] - lam * normed[:, tile:]).T
    o_ref[...] = (_rms(o, nw_ref[...]) * (1.0 - lambda_init)).astype(o_ref.dtype)


def _alibi_slopes(n):
    start = 2.0 ** (-8.0 / n)
    return np.array([start ** (i + 1) for i in range(n)], dtype=np.float32)


def _alibi_key_columns(tile):
    target = (LOG2E * _alibi_slopes(DIFF_HEADS))[:, None] * np.arange(tile, dtype=np.float32)[None, :]
    rest = jnp.asarray(target, F32)
    pieces = []
    for _ in range(ALIBI_PIECES):
        piece = rest.astype(BF16)
        pieces.append(piece)
        rest = rest - piece.astype(F32)
    cols = jnp.stack(pieces, axis=-1)
    return jnp.pad(cols, ((0, 0), (0, 0), (0, LANES - ALIBI_PIECES)))


def _diff_attention(qt, k, vt, lam_vecs, subln, lambda_init, batch, fuse_max, mxu_denominator, bf16_exp):
    nblk, _, tile = qt.shape
    nq = nblk // batch
    seq = nq * tile
    return pl.pallas_call(
        functools.partial(_diff_attn_kernel, lambda_init=lambda_init, fuse_max=fuse_max,
                          mxu_denominator=mxu_denominator, bf16_exp=bf16_exp),
        grid=(batch, DIFF_HEADS, nq),
        in_specs=[pl.BlockSpec(memory_space=pltpu.SMEM),
                  pl.BlockSpec((None, LANES, tile), lambda b, h, i: (b * nq + i, h, 0)),
                  pl.BlockSpec((seq, LANES), lambda b, h, i: (b, h)),
                  pl.BlockSpec((None, tile, LANES), lambda b, h, i: (h, 0, 0)),
                  pl.BlockSpec((nq, LANES, tile), lambda b, h, i: (b, h, 0)),
                  _resident((4, DIFF_HEAD_DIM)), _resident((1, DIFF_V_DIM))],
        out_specs=pl.BlockSpec((tile, LANES), lambda b, h, i: (b * nq + i, h)),
        out_shape=jax.ShapeDtypeStruct((batch * seq, DIFF_HEADS * DIFF_V_DIM), BF16),
        scratch_shapes=[pltpu.VMEM((DIFF_V_DIM + DENOM_ROWS, 2 * tile), F32), pltpu.VMEM((SUBLANES, 2 * tile), F32),
                        pltpu.VMEM((SUBLANES, 2 * tile), F32), pltpu.VMEM((2 * LANES, 2 * tile), BF16),
                        pltpu.VMEM((tile, 2 * tile), F32), pltpu.VMEM((tile, 2 * tile), F32),
                        pltpu.VMEM((SUBLANES, 2 * tile), F32), pltpu.VMEM((SUBLANES, 2 * tile), F32)],
        compiler_params=_params(("arbitrary", "arbitrary", "arbitrary")),
        name="diff_attention",
    )(jnp.asarray(_alibi_slopes(DIFF_HEADS)), qt, k, _alibi_key_columns(tile), vt, lam_vecs, subln[None, :])


def _cross_attn_kernel(x_ref, g_ref, wq_ref, k_ref, v_ref, wo_ref, o_ref, q_buf, att_buf):
    heads = [slice(h * X_HEAD_DIM, (h + 1) * X_HEAD_DIM) for h in range(X_HEADS)]
    xn = _rms(x_ref[...], g_ref[...]).astype(BF16)
    for c0 in range(0, D_MODEL, MATMUL_COLS):
        cs = slice(c0, c0 + MATMUL_COLS)
        q_buf[:, cs] = _dot(xn, wq_ref[:, cs]).astype(BF16)
    scores = [_dot_nt(q_buf[:, hs], k_ref[:, hs]) * (X_HEAD_DIM ** -0.5) for hs in heads]
    for hs, s in zip(heads, scores):
        p = jnp.exp(s - jnp.max(s, axis=-1, keepdims=True))
        pv = _dot(p.astype(BF16), v_ref[:, hs])
        att_buf[:, hs] = (pv / jnp.sum(p, axis=-1, keepdims=True)).astype(BF16)
    for c0 in range(0, D_MODEL, MATMUL_COLS):
        cs = slice(c0, c0 + MATMUL_COLS)
        o_ref[:, cs] = x_ref[:, cs] + _dot(att_buf[...], wo_ref[:, cs])


def _cross_attention(x, gain, wq, k_mem, v_mem, wo, batch):
    t = x.shape[0]
    seq = t // batch
    tm = min(CROSS_TILE, seq)
    per = seq // tm
    n_mem = k_mem.shape[0] // batch
    row = pl.BlockSpec((tm, D_MODEL), lambda i: (i, 0))
    mem = pl.BlockSpec((n_mem, D_MODEL), lambda i: (i // per, 0))
    return pl.pallas_call(
        _cross_attn_kernel,
        grid=(t // tm,),
        in_specs=[row, _resident((1, D_MODEL)), _resident(wq.shape), mem, mem, _resident(wo.shape)],
        out_specs=row,
        out_shape=jax.ShapeDtypeStruct((t, D_MODEL), F32),
        scratch_shapes=[pltpu.VMEM((tm, D_MODEL), BF16), pltpu.VMEM((tm, D_MODEL), BF16)],
        compiler_params=_params(("parallel",)),
        name="cross_attention",
    )(x, gain, wq, k_mem, v_mem, wo)


def _mlp_kernel(x_ref, g_ref, w1_ref, w2_ref, fg_ref, o_ref, acc_ref, *, final_norm):
    x = x_ref[...]
    xn = _rms(x, g_ref[...]).astype(BF16)
    acc_ref[...] = x
    for c0 in range(0, D_FF, MATMUL_COLS):
        hid = jnp.maximum(_dot(xn, w1_ref[:, c0:c0 + MATMUL_COLS]), 0.0)
        acc_ref[...] += _dot((hid * hid).astype(BF16), w2_ref[c0:c0 + MATMUL_COLS, :])
    out = acc_ref[...]
    if final_norm:
        out = _rms(out, fg_ref[...])
    o_ref[...] = out


def _mlp(x, gain, w1, w2, final_gain, final_norm, tile):
    t = x.shape[0]
    tm = min(tile, t)
    row = pl.BlockSpec((tm, D_MODEL), lambda i: (i, 0))
    return pl.pallas_call(
        functools.partial(_mlp_kernel, final_norm=final_norm),
        grid=(t // tm,),
        in_specs=[row, _resident((1, D_MODEL)), _resident(w1.shape), _resident(w2.shape), _resident((1, D_MODEL))],
        out_specs=row,
        out_shape=jax.ShapeDtypeStruct((t, D_MODEL), F32),
        scratch_shapes=[pltpu.VMEM((tm, D_MODEL), F32)],
        compiler_params=_params(("parallel",)),
        name="mlp",
    )(x, gain, w1, w2, final_gain)


def _even_weights(w_in, gla_w2):
    z_end = SSD_INNER
    xbc_end = z_end + SSD_XBC
    dt_end = xbc_end + SSD_HEADS
    v_end = dt_end + 2 * GLA_KEY + GLA_VAL
    glr_end = v_end + GLA_RANK
    w_main = jnp.concatenate([w_in[:, :xbc_end], w_in[:, dt_end:v_end], w_in[:, glr_end:]], axis=1).astype(BF16)
    w_small = jnp.concatenate([w_in[:, xbc_end:dt_end], w_in[:, v_end:glr_end],
                               jnp.zeros((D_MODEL, LANES - SSD_HEADS - GLA_RANK), w_in.dtype)], axis=1).astype(BF16)
    w2_pad = jnp.zeros((LANES, GLA_KEY), F32).at[SSD_HEADS:SSD_HEADS + GLA_RANK].set(gla_w2).astype(BF16)
    return w_main, w_small, w2_pad


def kernel(x, mem, ev_norm, ev_w_in, ev_conv_w, ev_conv_b, ev_dt_bias, ev_a_log, ev_d_skip, ev_ssd_norm, ev_gla_w2, ev_gla_b, ev_gla_norm, ev_w_out, od_norm, od_w_qkv, od_lam_q1, od_lam_k1, od_lam_q2, od_lam_k2, od_subln, od_w_o, xa_norm, xa_mem_norm, xa_wq, xa_wkv, xa_wo, mlp_norm, mlp_w1, mlp_w2, final_norm):
    batch, seq, d = x.shape
    n_mem = mem.shape[1]
    n_layers = xa_norm.shape[0]
    xf = x.reshape(batch * seq, d)
    memf = mem.reshape(batch * n_mem, d)
    for layer in range(n_layers):
        i = layer // 2
        tail_tile = 2 * TOKEN_TILE
        if layer % 2 == 0:
            w_main, w_small, w2_pad = _even_weights(ev_w_in[i], ev_gla_w2[i])
            z, xbc, q, k, v, r, small, logg = _even_in_proj(xf, ev_norm[i][None, :], w_main, w_small, w2_pad,
                                                            ev_gla_b[i][None, :])
            y = _ssd_scan(xbc, z, small, ev_conv_w[i], ev_conv_b[i], ev_dt_bias[i], ev_a_log[i], ev_d_skip[i],
                          ev_ssd_norm[i], batch, 1)
            o = _gla_scan(q, k, v, logg, r, ev_gla_norm[i], batch, 8 if i == 0 else 16)
            xf = _proj_residual(xf, [y, o], ev_w_out[i].astype(BF16), tail_tile)
        else:
            lambda_init = 0.8 - 0.6 * math.exp(-0.3 * layer)
            w_qkv = od_w_qkv[i].astype(BF16)
            qt, k, vt = _qkv_proj(xf, od_norm[i][None, :], w_qkv[:, :DIFF_QK].T, w_qkv[:, DIFF_QK:2 * DIFF_QK],
                                  w_qkv[:, 2 * DIFF_QK:].T, min(ATTN_TILE, seq))
            lam_vecs = jnp.stack([od_lam_q1[i], od_lam_k1[i], od_lam_q2[i], od_lam_k2[i]]).astype(F32)
            att = _diff_attention(qt, k, vt, lam_vecs, od_subln[i], lambda_init, batch, True, True, i == 1)
            xf = _proj_residual(xf, [att], od_w_o[i].astype(BF16), tail_tile)
        k_mem, v_mem = _norm_proj(memf, xa_mem_norm[layer][None, :], xa_wkv[layer].astype(BF16), (d, d))
        xf = _cross_attention(xf, xa_norm[layer][None, :], xa_wq[layer].astype(BF16), k_mem, v_mem,
                              xa_wo[layer].astype(BF16), batch)
        xf = _mlp(xf, mlp_norm[layer][None, :], mlp_w1[layer].astype(BF16), mlp_w2[layer].astype(BF16),
                  final_norm[None, :], layer == n_layers - 1, tail_tile)
    return xf.reshape(batch, seq, d)
```

```python
import functools
import math

import numpy as np
import jax
import jax.numpy as jnp
from jax import lax
from jax.experimental import pallas as pl
from jax.experimental.pallas import tpu as pltpu

F32 = jnp.float32
BF16 = jnp.bfloat16
EPS = 1e-5
NEG_BIG = -1e30

D_MODEL = 1024
N_LAYERS = 4

SSD_HEADS = 16
SSD_HEAD_DIM = 64
SSD_INNER = SSD_HEADS * SSD_HEAD_DIM
SSD_GROUPS = 2
SSD_STATE = 64
SSD_CONV = 4
SSD_CHUNK = 128
SSD_XBC = SSD_INNER + 2 * SSD_GROUPS * SSD_STATE
SSD_GROUP_WIDTH = SSD_INNER // SSD_GROUPS
SSD_GROUP_HEADS = SSD_HEADS // SSD_GROUPS

GLA_HEADS = 4
GLA_KEY = 512
GLA_VAL = 1024
GLA_HK = GLA_KEY // GLA_HEADS
GLA_HV = GLA_VAL // GLA_HEADS
GLA_RANK = 16
GLA_TAU = 16.0
GLA_CHUNK = 64
GLA_CHUNKS_PER_STEP = 8
GLA_SUB = 16
GLA_SAFE_LOG2 = 100.0

DIFF_HEADS = 8
DIFF_HEAD_DIM = 64
DIFF_V_DIM = 2 * DIFF_HEAD_DIM
DIFF_QK = DIFF_HEADS * 2 * DIFF_HEAD_DIM
LOG2E = 1.4426950408889634
DIFF_Q_SCALE = DIFF_HEAD_DIM ** -0.5 * LOG2E
ALIBI_PIECES = 3
DENOM_ROWS = 16
SPLIT_PIECES = 3

X_HEADS = 4
X_HEAD_DIM = D_MODEL // X_HEADS
D_FF = 4 * D_MODEL

LANES = 128
SUBLANES = 8
VMEM_LIMIT = 56 * 1024 * 1024
TOKEN_TILE = 512
ATTN_TILE = 1024
CROSS_TILE = 1024
MATMUL_COLS = 512
CONV_COLS = 256


def _resident(shape):
    nd = len(shape)
    return pl.BlockSpec(shape, lambda *_: (0,) * nd, pipeline_mode=pl.Buffered(1))


def _params(semantics):
    return pltpu.CompilerParams(dimension_semantics=semantics, vmem_limit_bytes=VMEM_LIMIT)


def _rms(x, gain):
    return x * lax.rsqrt(jnp.mean(x * x, axis=-1, keepdims=True) + EPS) * gain


def _sigmoid(x):
    return 1.0 / (1.0 + jnp.exp(-x))


def _softplus(x):
    return jnp.maximum(x, 0.0) + jnp.log1p(jnp.exp(-jnp.abs(x)))


def _dot(a, b):
    return jnp.dot(a, b, preferred_element_type=F32)


def _dot_nt(a, b):
    return lax.dot_general(a, b, (((1,), (1,)), ((), ())), preferred_element_type=F32)


def _dot_tn(a, b):
    return lax.dot_general(a, b, (((0,), (0,)), ((), ())), preferred_element_type=F32)


def _split3(x):
    pieces = []
    rest = x
    for _ in range(SPLIT_PIECES):
        piece = rest.astype(BF16)
        pieces.append(piece)
        rest = rest - piece.astype(F32)
    return pieces


def _silu(x):
    return x * _sigmoid(x)


def _proj_cols(xn, w_ref, off, width, o_ref, post=None, rows=None):
    for c0 in range(0, width, MATMUL_COLS):
        cw = min(MATMUL_COLS, width - c0)
        acc = _dot(xn, w_ref[:, off + c0:off + c0 + cw])
        if post is not None:
            acc = post(acc)
        if rows is None:
            o_ref[:, c0:c0 + cw] = acc.astype(o_ref.dtype)
        else:
            o_ref[rows, c0:c0 + cw] = acc.astype(o_ref.dtype)


_EVEN_MAIN = (SSD_INNER, SSD_XBC, GLA_KEY, GLA_KEY, GLA_VAL, GLA_VAL)


def _even_in_kernel(x_ref, g_ref, w_ref, ws_ref, w2_ref, gb_ref, cw_ref, cb_ref,
                    zg_ref, u_ref, q_ref, k_ref, v_ref, rg_ref, small_ref, logg_ref, conv_buf, *, tiles_per_seq):
    tm = x_ref.shape[0]

    @pl.when(pl.program_id(0) % tiles_per_seq == 0)
    def _():
        conv_buf[0:SUBLANES, :] = jnp.zeros((SUBLANES, SSD_XBC), F32)

    xn = _rms(x_ref[...], g_ref[...]).astype(BF16)
    offsets = dict(zip("z xbc q k v r".split(), (int(o) for o in np.cumsum((0,) + _EVEN_MAIN)[:-1])))
    widths = dict(zip("z xbc q k v r".split(), _EVEN_MAIN))

    def conv_piece(c0):
        cs = slice(c0, c0 + CONV_COLS)
        acc = cb_ref[:, cs]
        for tap in range(SSD_CONV):
            start = SUBLANES - (SSD_CONV - 1) + tap
            acc = acc + cw_ref[tap:tap + 1, cs] * conv_buf[start:start + tm, cs]
        conv_buf[0:SUBLANES, cs] = conv_buf[tm:tm + SUBLANES, cs]
        u_ref[:, cs] = _silu(acc).astype(u_ref.dtype)

    _proj_cols(xn, w_ref, offsets["xbc"], widths["xbc"], conv_buf, rows=slice(SUBLANES, SUBLANES + tm))
    conv_starts = list(range(0, SSD_XBC, CONV_COLS))
    for name, o_ref, post in (("z", zg_ref, _silu), ("q", q_ref, lambda a: a * GLA_HK ** -0.5), ("k", k_ref, None),
                              ("v", v_ref, None), ("r", rg_ref, _silu)):
        _proj_cols(xn, w_ref, offsets[name], widths[name], o_ref, post)
        if conv_starts:
            conv_piece(conv_starts.pop(0))
    for c0 in conv_starts:
        conv_piece(c0)
    small = _dot(xn, ws_ref[...])
    small_ref[...] = small
    logits = _dot(small.astype(BF16), w2_ref[...]) + gb_ref[...]
    logg_ref[...] = -_softplus(-logits) * (LOG2E / GLA_TAU)


def _even_in_proj(x, gain, w_main, w_small, w2_pad, gla_b, conv_w, conv_b, seq):
    t = x.shape[0]
    tm = min(TOKEN_TILE, seq)
    row = lambda w: pl.BlockSpec((tm, w), lambda i: (i, 0))
    out_shape = [jax.ShapeDtypeStruct((t, w), BF16) for w in _EVEN_MAIN]
    out_shape += [jax.ShapeDtypeStruct((t, LANES), F32), jax.ShapeDtypeStruct((t, GLA_KEY), F32)]
    return pl.pallas_call(
        functools.partial(_even_in_kernel, tiles_per_seq=seq // tm),
        grid=(t // tm,),
        in_specs=[row(D_MODEL), _resident((1, D_MODEL)), _resident(w_main.shape), _resident(w_small.shape),
                  _resident(w2_pad.shape), _resident((1, GLA_KEY)), _resident((SSD_CONV, SSD_XBC)),
                  _resident((1, SSD_XBC))],
        out_specs=[row(w) for w in _EVEN_MAIN] + [row(LANES), row(GLA_KEY)],
        out_shape=out_shape,
        scratch_shapes=[pltpu.VMEM((tm + SUBLANES, SSD_XBC), F32)],
        compiler_params=_params(("arbitrary",)),
        name="even_in_proj",
    )(x, gain, w_main, w_small, w2_pad, gla_b, conv_w, conv_b[None, :])


def _ssd_kernel(u_ref, zg_ref, small_ref, dtb_ref, alog_ref, dskip_ref, nw_ref, expand_ref,
                tri_ref, y_ref, state_ref, y_buf, *, chunks):
    @pl.when(pl.program_id(1) == 0)
    def _():
        state_ref[...] = jnp.zeros_like(state_ref)

    for c in range(chunks):
        rows = pl.ds(c * SSD_CHUNK, SSD_CHUNK)
        _ssd_chunk(u_ref.at[rows], zg_ref.at[rows], small_ref.at[rows], dtb_ref, alog_ref,
                   dskip_ref, nw_ref, expand_ref, tri_ref, y_ref.at[rows], state_ref, y_buf)


def _ssd_chunk(u_ref, zg_ref, small_ref, dtb_ref, alog_ref, dskip_ref, nw_ref, expand_ref,
               tri_ref, y_ref, state_ref, y_buf):
    q = SSD_CHUNK
    u = u_ref[...].astype(F32)
    xs = u[:, :SSD_INNER]
    bm = u[:, SSD_INNER:SSD_INNER + SSD_GROUPS * SSD_STATE]
    cm = u[:, SSD_INNER + SSD_GROUPS * SSD_STATE:]
    bm_t = bm.T

    lane = lax.broadcasted_iota(jnp.int32, (q, LANES), 1)
    dt = jnp.where(lane < SSD_HEADS, _softplus(small_ref[...] + dtb_ref[...]), 0.0)
    loga = dt * (-jnp.exp(alog_ref[...]) * LOG2E)
    cum = _dot(tri_ref[...], jnp.concatenate(_split3(loga), axis=0))
    cum_row = cum.T
    expand = expand_ref[...]
    dt_full = _dot(jnp.concatenate(_split3(dt), axis=1), expand)
    cum_full = _dot(jnp.concatenate(_split3(cum), axis=1), expand)
    last_full = cum_full[q - 1:q, :]
    xdt = xs * dt_full
    xdt_b = xdt.astype(BF16)
    tril = lax.broadcasted_iota(jnp.int32, (q, q), 0) >= lax.broadcasted_iota(jnp.int32, (q, q), 1)

    for g in range(SSD_GROUPS):
        gs = slice(g * SSD_GROUP_WIDTH, (g + 1) * SSD_GROUP_WIDTH)
        cg = cm[:, g * SSD_STATE:(g + 1) * SSD_STATE].astype(BF16)
        bg = bm[:, g * SSD_STATE:(g + 1) * SSD_STATE].astype(BF16)
        scores = _dot_nt(cg, bg)
        st = state_ref[g]
        y_inter = _dot(cg, st.astype(BF16)) * jnp.exp2(cum_full[:, gs])
        for e in range(SSD_GROUP_HEADS):
            h = g * SSD_GROUP_HEADS + e
            hs = slice(h * SSD_HEAD_DIM, (h + 1) * SSD_HEAD_DIM)
            seg = cum[:, h:h + 1] - cum_row[h:h + 1, :]
            decay = jnp.exp2(jnp.where(tril, seg, NEG_BIG))
            ye = _dot((scores * decay).astype(BF16), xdt_b[:, hs])
            y_buf[:, hs] = ye + y_inter[:, e * SSD_HEAD_DIM:(e + 1) * SSD_HEAD_DIM]
        wts = jnp.exp2(last_full[:, gs] - cum_full[:, gs])
        xw = (xdt[:, gs] * wts).astype(BF16)
        bg_t = bm_t[g * SSD_STATE:(g + 1) * SSD_STATE, :].astype(BF16)
        state_ref[g] = st * jnp.exp2(last_full[:, gs]) + _dot(bg_t, xw)

    y = (y_buf[...] + dskip_ref[...] * xs) * zg_ref[...].astype(F32)
    for g in range(SSD_GROUPS):
        gs = slice(g * SSD_GROUP_WIDTH, (g + 1) * SSD_GROUP_WIDTH)
        y_ref[:, gs] = _rms(y[:, gs], nw_ref[:, gs]).astype(y_ref.dtype)


def _ssd_scan(u, zg, small, dt_bias, a_log, d_skip, norm_w, batch, chunks):
    t = u.shape[0]
    q = SSD_CHUNK
    nc = t // batch // (q * chunks)
    pad = lambda v: jnp.pad(v.astype(F32), (0, LANES - v.shape[0]))[None, :]
    expand = (np.arange(LANES)[:, None] == (np.arange(SSD_INNER)[None, :] // SSD_HEAD_DIM)).astype(np.float32)
    expand = np.tile(expand, (SPLIT_PIECES, 1))
    tri = np.tile(np.tril(np.ones((q, q), np.float32)), (1, SPLIT_PIECES))
    row = lambda w: pl.BlockSpec((q * chunks, w), lambda b, c: (b * nc + c, 0))
    return pl.pallas_call(
        functools.partial(_ssd_kernel, chunks=chunks),
        grid=(batch, nc),
        in_specs=[row(SSD_XBC), row(SSD_INNER), row(LANES),
                  _resident((1, LANES)), _resident((1, LANES)), _resident((1, SSD_INNER)), _resident((1, SSD_INNER)),
                  _resident((SPLIT_PIECES * LANES, SSD_INNER)), _resident((q, SPLIT_PIECES * q))],
        out_specs=row(SSD_INNER),
        out_shape=jax.ShapeDtypeStruct((t, SSD_INNER), BF16),
        scratch_shapes=[pltpu.VMEM((SSD_GROUPS, SSD_STATE, SSD_GROUP_WIDTH), F32),
                        pltpu.VMEM((q, SSD_INNER), F32)],
        compiler_params=_params(("arbitrary", "arbitrary")),
        name="ssd_scan",
    )(u, zg, small, pad(dt_bias), pad(a_log),
      jnp.repeat(d_skip.astype(F32), SSD_HEAD_DIM)[None, :], norm_w[None, :], jnp.asarray(expand, BF16),
      jnp.asarray(tri, BF16))


def _gla_kernel(q_ref, k_ref, v_ref, g_ref, r_ref, nw_ref, tri_ref, o_ref,
                state_ref, k_pad, c_pad, v_pad, term_buf, intra_buf, *, chunks):
    @pl.when(pl.program_id(1) == 0)
    def _():
        state_ref[...] = jnp.zeros_like(state_ref)

    for c in range(chunks):
        rows = pl.ds(c * GLA_CHUNK, GLA_CHUNK)
        _gla_chunk(q_ref.at[rows], k_ref.at[rows], v_ref.at[rows], g_ref.at[rows], r_ref.at[rows], nw_ref, tri_ref,
                   o_ref.at[rows], state_ref, k_pad, c_pad, v_pad, term_buf, intra_buf)


def _gla_chunk(q_ref, k_ref, v_ref, g_ref, r_ref, nw_ref, tri_ref, o_ref,
               state_ref, k_pad, c_pad, v_pad, term_buf, intra_buf):
    n = GLA_CHUNK
    sub = GLA_SUB
    nsub = n // sub
    heads = [(slice(h * GLA_HK, (h + 1) * GLA_HK), slice(h * GLA_HV, (h + 1) * GLA_HV)) for h in range(GLA_HEADS)]
    row = lax.broadcasted_iota(jnp.int32, (n, n), 0)
    col = lax.broadcasted_iota(jnp.int32, (n, n), 1)
    cums = [_dot(tri_ref[...], jnp.concatenate(_split3(g_ref[:, ks]), axis=0)) for ks, _ in heads]

    def sub_ref(cum, i):
        return cum[i * sub - 1:i * sub, :] if i > 0 else jnp.zeros((1, GLA_HK), F32)

    def factored_blocks(h, first, clamp):
        ks, _ = heads[h]
        cum = cums[h]
        qh = q_ref[:, ks].astype(F32)
        kh = k_ref[:, ks].astype(F32)
        blocks = [jnp.zeros((sub, n), F32)] * first
        for i in range(first, nsub):
            ref = sub_ref(cum, i)
            qt = qh[i * sub:(i + 1) * sub, :] * jnp.exp2(cum[i * sub:(i + 1) * sub, :] - ref)
            kt = kh * jnp.exp2(jnp.minimum(ref - cum, clamp))
            blocks.append(_dot_nt(qt.astype(BF16), kt.astype(BF16)))
        return jnp.concatenate(blocks, axis=0)

    worst = jnp.max(jnp.concatenate([-cum[n - 1:n, :] for cum in cums], axis=0))
    safe = worst <= GLA_SAFE_LOG2

    @pl.when(safe)
    def _():
        for h, (ks, vs) in enumerate(heads):
            cum = cums[h]
            qd = q_ref[:, ks].astype(F32) * jnp.exp2(cum)
            kd = k_ref[:, ks].astype(F32) * jnp.exp2(-cum)
            a = jnp.where(row >= col, _dot_nt(qd.astype(BF16), kd.astype(BF16)), 0.0)
            intra_buf[:, vs] = _dot(a.astype(BF16), v_ref[:, vs])

    @pl.when(jnp.logical_not(safe))
    def _():
        k_pad[0:sub, :] = jnp.zeros((sub, GLA_HK), F32)
        c_pad[0:sub, :] = jnp.zeros((sub, GLA_HK), F32)
        v_pad[0:sub, :] = jnp.zeros((sub, GLA_HV), F32)
        ones = jnp.ones((GLA_HK, GLA_HV), BF16)
        for h, (ks, vs) in enumerate(heads):
            cum = cums[h]
            qh = q_ref[:, ks].astype(F32)
            vh_b = v_ref[:, vs]
            a_far = jnp.where((row - col) >= sub, factored_blocks(h, 1, 0.0), 0.0)
            o = _dot(a_far.astype(BF16), vh_b)
            k_pad[sub:sub + n, :] = k_ref[:, ks].astype(F32)
            c_pad[sub:sub + n, :] = cum
            v_pad[sub:sub + n, :] = vh_b.astype(F32)
            for d in range(sub):
                kd = k_pad[sub - d:sub - d + n, :]
                cd = c_pad[sub - d:sub - d + n, :]
                term_buf[d * n:(d + 1) * n, :] = (qh * kd * jnp.exp2(cum - cd)).astype(BF16)
            near = _dot(term_buf[...], ones)
            for d in range(sub):
                o = o + near[d * n:(d + 1) * n, :] * v_pad[sub - d:sub - d + n, :]
            intra_buf[:, vs] = o

    for h, (ks, vs) in enumerate(heads):
        cum = cums[h]
        last = cum[n - 1:n, :]
        qh = q_ref[:, ks].astype(F32)
        kh = k_ref[:, ks].astype(F32)
        vh_b = v_ref[:, vs]
        st = state_ref[h]
        o = intra_buf[:, vs] + _dot_nt((qh * jnp.exp2(cum)).astype(BF16), st.astype(BF16))
        state_ref[h] = st * jnp.exp2(last) + _dot_tn(vh_b, (kh * jnp.exp2(last - cum)).astype(BF16))
        o_ref[:, vs] = (_rms(o, nw_ref[...]) * r_ref[:, vs].astype(F32)).astype(o_ref.dtype)


def _gla_scan(q, k, v, logg, r, norm_w, batch, chunks):
    t = q.shape[0]
    n = GLA_CHUNK
    nc = t // batch // (n * chunks)
    tri = np.tile(np.tril(np.ones((n, n), np.float32)), (1, SPLIT_PIECES))
    row = lambda w: pl.BlockSpec((n * chunks, w), lambda b, c: (b * nc + c, 0))
    return pl.pallas_call(
        functools.partial(_gla_kernel, chunks=chunks),
        grid=(batch, nc),
        in_specs=[row(GLA_KEY), row(GLA_KEY), row(GLA_VAL), row(GLA_KEY), row(GLA_VAL),
                  _resident((1, GLA_HV)), _resident((n, SPLIT_PIECES * n))],
        out_specs=row(GLA_VAL),
        out_shape=jax.ShapeDtypeStruct((t, GLA_VAL), BF16),
        scratch_shapes=[pltpu.VMEM((GLA_HEADS, GLA_HV, GLA_HK), F32),
                        pltpu.VMEM((n + GLA_SUB, GLA_HK), F32),
                        pltpu.VMEM((n + GLA_SUB, GLA_HK), F32),
                        pltpu.VMEM((n + GLA_SUB, GLA_HV), F32),
                        pltpu.VMEM((GLA_SUB * n, GLA_HK), BF16),
                        pltpu.VMEM((n, GLA_VAL), F32)],
        compiler_params=_params(("arbitrary", "arbitrary")),
        name="gla_scan",
    )(q, k, v, logg, r, norm_w[None, :], jnp.asarray(tri, BF16))


def _proj_residual_kernel(*refs, widths):
    x_ref = refs[0]
    a_refs = refs[1:1 + len(widths)]
    w_ref = refs[1 + len(widths)]
    o_ref = refs[2 + len(widths)]
    for c0 in range(0, D_MODEL, MATMUL_COLS):
        cs = slice(c0, c0 + MATMUL_COLS)
        acc = x_ref[:, cs]
        off = 0
        for a_ref, width in zip(a_refs, widths):
            acc = acc + _dot(a_ref[...], w_ref[off:off + width, cs])
            off += width
        o_ref[:, cs] = acc


def _proj_residual(x, acts, w, tile):
    t = x.shape[0]
    tm = min(tile, t)
    widths = tuple(a.shape[1] for a in acts)
    row = lambda wd: pl.BlockSpec((tm, wd), lambda i: (i, 0))
    return pl.pallas_call(
        functools.partial(_proj_residual_kernel, widths=widths),
        grid=(t // tm,),
        in_specs=[row(D_MODEL)] + [row(wd) for wd in widths] + [_resident(w.shape)],
        out_specs=row(D_MODEL),
        out_shape=jax.ShapeDtypeStruct((t, D_MODEL), F32),
        compiler_params=_params(("parallel",)),
        name="proj_residual",
    )(x, *acts, w)


def _norm_proj_kernel(x_ref, g_ref, w_ref, *o_refs):
    xn = _rms(x_ref[...], g_ref[...]).astype(BF16)
    off = 0
    for o_ref in o_refs:
        width = o_ref.shape[1]
        _proj_cols(xn, w_ref, off, width, o_ref)
        off += width


def _norm_proj(x, gain, w, widths):
    t = x.shape[0]
    tm = min(TOKEN_TILE, t)
    row = lambda wd: pl.BlockSpec((tm, wd), lambda i: (i, 0))
    return pl.pallas_call(
        _norm_proj_kernel,
        grid=(t // tm,),
        in_specs=[row(D_MODEL), _resident((1, D_MODEL)), _resident(w.shape)],
        out_specs=[row(wd) for wd in widths],
        out_shape=[jax.ShapeDtypeStruct((t, wd), BF16) for wd in widths],
        compiler_params=_params(("parallel",)),
        name="norm_proj",
    )(x, gain, w)


def _qkv_proj_kernel(x_ref, g_ref, wqt_ref, wk_ref, wvt_ref, qt_ref, k_ref, vt_ref):
    xn = _rms(x_ref[...], g_ref[...]).astype(BF16)
    for c0 in range(0, DIFF_QK, MATMUL_COLS):
        cs = slice(c0, c0 + MATMUL_COLS)
        qt_ref[cs, :] = (_dot_nt(wqt_ref[cs, :], xn) * DIFF_Q_SCALE).astype(BF16)
        vt_ref[cs, :] = _dot_nt(wvt_ref[cs, :], xn).astype(BF16)
    _proj_cols(xn, wk_ref, 0, DIFF_QK, k_ref)


def _qkv_proj(x, gain, wq_t, wk, wv_t, tm):
    t = x.shape[0]
    row = lambda wd: pl.BlockSpec((tm, wd), lambda i: (i, 0))
    tr = pl.BlockSpec((None, DIFF_QK, tm), lambda i: (i, 0, 0))
    tr_shape = jax.ShapeDtypeStruct((t // tm, DIFF_QK, tm), BF16)
    return pl.pallas_call(
        _qkv_proj_kernel,
        grid=(t // tm,),
        in_specs=[row(D_MODEL), _resident((1, D_MODEL)), _resident(wq_t.shape), _resident(wk.shape),
                  _resident(wv_t.shape)],
        out_specs=[tr, row(DIFF_QK), tr],
        out_shape=[tr_shape, jax.ShapeDtypeStruct((t, DIFF_QK), BF16), tr_shape],
        compiler_params=_params(("parallel",)),
        name="qkv_proj",
    )(x, gain, wq_t, wk, wv_t)


def _diff_attn_kernel(slope_ref, qt_ref, k_ref, kx_ref, vt_ref, lam_ref, nw_ref, o_ref,
                      acc_ref, m_ref, qa_ref, s_a, s_b, mx_a, mx_b, *, lambda_init):
    tile = qt_ref.shape[1]
    h = pl.program_id(1)
    i = pl.program_id(2)
    tile_step = slope_ref[h] * (LOG2E * tile)
    half = DIFF_HEAD_DIM

    qt = qt_ref[...]
    zeros = jnp.zeros((half, tile), BF16)
    ones_rows = (lax.broadcasted_iota(jnp.int32, (LANES, tile), 0) < ALIBI_PIECES).astype(BF16)
    qa_ref[:, :tile] = jnp.concatenate([qt[:half], zeros, ones_rows], axis=0)
    qa_ref[:, tile:] = jnp.concatenate([zeros, qt[half:], ones_rows], axis=0)
    acc_ref[...] = jnp.zeros_like(acc_ref)
    m_ref[...] = jnp.full_like(m_ref, NEG_BIG)

    def scores(t, buf, masked):
        s_ref, mx_ref = buf
        start = pl.multiple_of(t * tile, tile)
        k_aug = jnp.concatenate([k_ref[pl.ds(start, tile), :], kx_ref[...]], axis=1)
        s = _dot(k_aug, qa_ref[...])
        if masked:
            key_pos = lax.broadcasted_iota(jnp.int32, (tile, 2 * tile), 0)
            lane_pos = lax.broadcasted_iota(jnp.int32, (tile, 2 * tile), 1)
            causal = key_pos <= jnp.where(lane_pos >= tile, lane_pos - tile, lane_pos)
            s = jnp.where(causal, s, NEG_BIG)
        s_ref[...] = s
        mx_ref[...] = jnp.max(s.reshape(tile // SUBLANES, SUBLANES, 2 * tile), axis=0)

    def softmax_pv(t, buf):
        s_ref, mx_ref = buf
        off = -tile_step * (i - t).astype(F32)
        m_old = m_ref[0:1, :]
        m_new = jnp.maximum(m_old, jnp.max(mx_ref[...], axis=0, keepdims=True) + off)
        alpha = jnp.exp2(m_old - m_new)
        shift = off - m_new
        v_aug = jnp.concatenate([vt_ref[t], jnp.ones((DENOM_ROWS, tile), BF16)], axis=0)
        p = jnp.exp2(s_ref[...] + shift)
        acc_ref[...] = alpha * acc_ref[...] + _dot(v_aug, p.astype(BF16))
        m_ref[0:1, :] = m_new

    buf_a = (s_a, mx_a)
    buf_b = (s_b, mx_b)
    even = i % 2 == 0

    @pl.when(i == 0)
    def _():
        scores(0, buf_a, True)

    @pl.when(jnp.logical_and(even, i > 0))
    def _():
        scores(0, buf_a, False)
        scores(1, buf_b, False)
        softmax_pv(0, buf_a)

    @pl.when(jnp.logical_not(even))
    def _():
        scores(0, buf_b, False)

    first = jnp.where(even, 1, 0)

    def pair(u, carry):
        t = first + 2 * u
        scores(t + 1, buf_a, False)
        softmax_pv(t, buf_b)
        scores(t + 2, buf_b, False)
        softmax_pv(t + 1, buf_a)
        return carry

    lax.fori_loop(0, (i - 1) // 2, pair, 0)

    @pl.when(i > 0)
    def _():
        scores(i, buf_a, True)
        softmax_pv(i - 1, buf_b)

    softmax_pv(i, buf_a)

    lam = (jnp.exp(jnp.sum(lam_ref[0:1, :] * lam_ref[1:2, :], axis=-1, keepdims=True))
           - jnp.exp(jnp.sum(lam_ref[2:3, :] * lam_ref[3:4, :], axis=-1, keepdims=True)) + lambda_init)
    normed = acc_ref[0:DIFF_V_DIM, :] / acc_ref[DIFF_V_DIM:DIFF_V_DIM + 1, :]
    o = (normed[:, :tile] - lam * normed[:, tile:]).T
    o_ref[...] = (_rms(o, nw_ref[...]) * (1.0 - lambda_init)).astype(o_ref.dtype)


def _alibi_slopes(n):
    start = 2.0 ** (-8.0 / n)
    return np.array([start ** (i + 1) for i in range(n)], dtype=np.float32)


def _alibi_key_columns(tile):
    target = (LOG2E * _alibi_slopes(DIFF_HEADS))[:, None] * np.arange(tile, dtype=np.float32)[None, :]
    rest = jnp.asarray(target, F32)
    pieces = []
    for _ in range(ALIBI_PIECES):
        piece = rest.astype(BF16)
        pieces.append(piece)
        rest = rest - piece.astype(F32)
    cols = jnp.stack(pieces, axis=-1)
    return jnp.pad(cols, ((0, 0), (0, 0), (0, LANES - ALIBI_PIECES)))


def _diff_attention(qt, k, vt, lam_vecs, subln, lambda_init, batch):
    nblk, _, tile = qt.shape
    nq = nblk // batch
    seq = nq * tile
    return pl.pallas_call(
        functools.partial(_diff_attn_kernel, lambda_init=lambda_init),
        grid=(batch, DIFF_HEADS, nq),
        in_specs=[pl.BlockSpec(memory_space=pltpu.SMEM),
                  pl.BlockSpec((None, LANES, tile), lambda b, h, i: (b * nq + i, h, 0)),
                  pl.BlockSpec((seq, LANES), lambda b, h, i: (b, h)),
                  pl.BlockSpec((None, tile, LANES), lambda b, h, i: (h, 0, 0)),
                  pl.BlockSpec((nq, LANES, tile), lambda b, h, i: (b, h, 0)),
                  _resident((4, DIFF_HEAD_DIM)), _resident((1, DIFF_V_DIM))],
        out_specs=pl.BlockSpec((tile, LANES), lambda b, h, i: (b * nq + i, h)),
        out_shape=jax.ShapeDtypeStruct((batch * seq, DIFF_HEADS * DIFF_V_DIM), BF16),
        scratch_shapes=[pltpu.VMEM((DIFF_V_DIM + DENOM_ROWS, 2 * tile), F32), pltpu.VMEM((SUBLANES, 2 * tile), F32),
                        pltpu.VMEM((2 * LANES, 2 * tile), BF16),
                        pltpu.VMEM((tile, 2 * tile), F32), pltpu.VMEM((tile, 2 * tile), F32),
                        pltpu.VMEM((SUBLANES, 2 * tile), F32), pltpu.VMEM((SUBLANES, 2 * tile), F32)],
        compiler_params=_params(("arbitrary", "arbitrary", "arbitrary")),
        name="diff_attention",
    )(jnp.asarray(_alibi_slopes(DIFF_HEADS)), qt, k, _alibi_key_columns(tile), vt, lam_vecs, subln[None, :])


def _cross_attn_kernel(x_ref, g_ref, wq_ref, k_ref, v_ref, wo_ref, o_ref, q_buf, att_buf):
    heads = [slice(h * X_HEAD_DIM, (h + 1) * X_HEAD_DIM) for h in range(X_HEADS)]
    xn = _rms(x_ref[...], g_ref[...]).astype(BF16)
    for c0 in range(0, D_MODEL, MATMUL_COLS):
        cs = slice(c0, c0 + MATMUL_COLS)
        q_buf[:, cs] = _dot(xn, wq_ref[:, cs]).astype(BF16)
    scores = [_dot_nt(q_buf[:, hs], k_ref[:, hs]) * (X_HEAD_DIM ** -0.5) for hs in heads]
    for hs, s in zip(heads, scores):
        p = jnp.exp(s - jnp.max(s, axis=-1, keepdims=True))
        pv = _dot(p.astype(BF16), v_ref[:, hs])
        att_buf[:, hs] = (pv / jnp.sum(p, axis=-1, keepdims=True)).astype(BF16)
    for c0 in range(0, D_MODEL, MATMUL_COLS):
        cs = slice(c0, c0 + MATMUL_COLS)
        o_ref[:, cs] = x_ref[:, cs] + _dot(att_buf[...], wo_ref[:, cs])


def _cross_attention(x, gain, wq, k_mem, v_mem, wo, batch):
    t = x.shape[0]
    seq = t // batch
    tm = min(CROSS_TILE, seq)
    per = seq // tm
    n_mem = k_mem.shape[0] // batch
    row = pl.BlockSpec((tm, D_MODEL), lambda i: (i, 0))
    mem = pl.BlockSpec((n_mem, D_MODEL), lambda i: (i // per, 0))
    return pl.pallas_call(
        _cross_attn_kernel,
        grid=(t // tm,),
        in_specs=[row, _resident((1, D_MODEL)), _resident(wq.shape), mem, mem, _resident(wo.shape)],
        out_specs=row,
        out_shape=jax.ShapeDtypeStruct((t, D_MODEL), F32),
        scratch_shapes=[pltpu.VMEM((tm, D_MODEL), BF16), pltpu.VMEM((tm, D_MODEL), BF16)],
        compiler_params=_params(("parallel",)),
        name="cross_attention",
    )(x, gain, wq, k_mem, v_mem, wo)


def _mlp_kernel(x_ref, g_ref, w1_ref, w2_ref, fg_ref, o_ref, acc_ref, *, final_norm):
    x = x_ref[...]
    xn = _rms(x, g_ref[...]).astype(BF16)
    acc_ref[...] = x
    for c0 in range(0, D_FF, MATMUL_COLS):
        hid = jnp.maximum(_dot(xn, w1_ref[:, c0:c0 + MATMUL_COLS]), 0.0)
        acc_ref[...] += _dot((hid * hid).astype(BF16), w2_ref[c0:c0 + MATMUL_COLS, :])
    out = acc_ref[...]
    if final_norm:
        out = _rms(out, fg_ref[...])
    o_ref[...] = out


def _mlp(x, gain, w1, w2, final_gain, final_norm, tile):
    t = x.shape[0]
    tm = min(tile, t)
    row = pl.BlockSpec((tm, D_MODEL), lambda i: (i, 0))
    return pl.pallas_call(
        functools.partial(_mlp_kernel, final_norm=final_norm),
        grid=(t // tm,),
        in_specs=[row, _resident((1, D_MODEL)), _resident(w1.shape), _resident(w2.shape), _resident((1, D_MODEL))],
        out_specs=row,
        out_shape=jax.ShapeDtypeStruct((t, D_MODEL), F32),
        scratch_shapes=[pltpu.VMEM((tm, D_MODEL), F32)],
        compiler_params=_params(("parallel",)),
        name="mlp",
    )(x, gain, w1, w2, final_gain)


def _even_weights(w_in, gla_w2):
    z_end = SSD_INNER
    xbc_end = z_end + SSD_XBC
    dt_end = xbc_end + SSD_HEADS
    v_end = dt_end + 2 * GLA_KEY + GLA_VAL
    glr_end = v_end + GLA_RANK
    w_main = jnp.concatenate([w_in[:, :xbc_end], w_in[:, dt_end:v_end], w_in[:, glr_end:]], axis=1).astype(BF16)
    w_small = jnp.concatenate([w_in[:, xbc_end:dt_end], w_in[:, v_end:glr_end],
                               jnp.zeros((D_MODEL, LANES - SSD_HEADS - GLA_RANK), w_in.dtype)], axis=1).astype(BF16)
    w2_pad = jnp.zeros((LANES, GLA_KEY), F32).at[SSD_HEADS:SSD_HEADS + GLA_RANK].set(gla_w2).astype(BF16)
    return w_main, w_small, w2_pad


def kernel(x, mem, ev_norm, ev_w_in, ev_conv_w, ev_conv_b, ev_dt_bias, ev_a_log, ev_d_skip, ev_ssd_norm, ev_gla_w2, ev_gla_b, ev_gla_norm, ev_w_out, od_norm, od_w_qkv, od_lam_q1, od_lam_k1, od_lam_q2, od_lam_k2, od_subln, od_w_o, xa_norm, xa_mem_norm, xa_wq, xa_wkv, xa_wo, mlp_norm, mlp_w1, mlp_w2, final_norm):
    batch, seq, d = x.shape
    n_mem = mem.shape[1]
    n_layers = xa_norm.shape[0]
    xf = x.reshape(batch * seq, d)
    memf = mem.reshape(batch * n_mem, d)
    for layer in range(n_layers):
        i = layer // 2
        tail_tile = 2 * TOKEN_TILE
        if layer % 2 == 0:
            w_main, w_small, w2_pad = _even_weights(ev_w_in[i], ev_gla_w2[i])
            zg, u, q, k, v, rg, small, logg = _even_in_proj(xf, ev_norm[i][None, :], w_main, w_small, w2_pad,
                                                            ev_gla_b[i][None, :], ev_conv_w[i], ev_conv_b[i], seq)
            y = _ssd_scan(u, zg, small, ev_dt_bias[i], ev_a_log[i], ev_d_skip[i], ev_ssd_norm[i], batch, 1)
            o = _gla_scan(q, k, v, logg, rg, ev_gla_norm[i], batch, GLA_CHUNKS_PER_STEP)
            xf = _proj_residual(xf, [y, o], ev_w_out[i].astype(BF16), tail_tile)
        else:
            lambda_init = 0.8 - 0.6 * math.exp(-0.3 * layer)
            w_qkv = od_w_qkv[i].astype(BF16)
            qt, k, vt = _qkv_proj(xf, od_norm[i][None, :], w_qkv[:, :DIFF_QK].T, w_qkv[:, DIFF_QK:2 * DIFF_QK],
                                  w_qkv[:, 2 * DIFF_QK:].T, min(ATTN_TILE, seq))
            lam_vecs = jnp.stack([od_lam_q1[i], od_lam_k1[i], od_lam_q2[i], od_lam_k2[i]]).astype(F32)
            att = _diff_attention(qt, k, vt, lam_vecs, od_subln[i], lambda_init, batch)
            xf = _proj_residual(xf, [att], od_w_o[i].astype(BF16), tail_tile)
        k_mem, v_mem = _norm_proj(memf, xa_mem_norm[layer][None, :], xa_wkv[layer].astype(BF16), (d, d))
        xf = _cross_attention(xf, xa_norm[layer][None, :], xa_wq[layer].astype(BF16), k_mem, v_mem,
                              xa_wo[layer].astype(BF16), batch)
        xf = _mlp(xf, mlp_norm[layer][None, :], mlp_w1[layer].astype(BF16), mlp_w2[layer].astype(BF16),
                  final_norm[None, :], layer == n_layers - 1, tail_tile)
    return xf.reshape(batch, seq, d)
```

```python
import functools
import math

import numpy as np
import jax
import jax.numpy as jnp
from jax import lax
from jax.experimental import pallas as pl
from jax.experimental.pallas import tpu as pltpu

F32 = jnp.float32
BF16 = jnp.bfloat16
EPS = 1e-5
NEG_BIG = -1e30

D_MODEL = 1024
N_LAYERS = 4

SSD_HEADS = 16
SSD_HEAD_DIM = 64
SSD_INNER = SSD_HEADS * SSD_HEAD_DIM
SSD_GROUPS = 2
SSD_STATE = 64
SSD_CONV = 4
SSD_CHUNK = 128
SSD_XBC = SSD_INNER + 2 * SSD_GROUPS * SSD_STATE
SSD_GROUP_WIDTH = SSD_INNER // SSD_GROUPS
SSD_GROUP_HEADS = SSD_HEADS // SSD_GROUPS

GLA_HEADS = 4
GLA_KEY = 512
GLA_VAL = 1024
GLA_HK = GLA_KEY // GLA_HEADS
GLA_HV = GLA_VAL // GLA_HEADS
GLA_RANK = 16
GLA_TAU = 16.0
GLA_CHUNK = 64
GLA_CHUNKS_PER_STEP = 8
GLA_SUB = 16
GLA_SAFE_LOG2 = 100.0

DIFF_HEADS = 8
DIFF_HEAD_DIM = 64
DIFF_V_DIM = 2 * DIFF_HEAD_DIM
DIFF_QK = DIFF_HEADS * 2 * DIFF_HEAD_DIM
LOG2E = 1.4426950408889634
DIFF_Q_SCALE = DIFF_HEAD_DIM ** -0.5 * LOG2E
ALIBI_PIECES = 3
DENOM_ROWS = 16
SPLIT_PIECES = 3

X_HEADS = 4
X_HEAD_DIM = D_MODEL // X_HEADS
D_FF = 4 * D_MODEL

LANES = 128
SUBLANES = 8
VMEM_LIMIT = 56 * 1024 * 1024
TOKEN_TILE = 512
ATTN_TILE = 1024
CROSS_TILE = 1024
MATMUL_COLS = 512
CONV_COLS = 256


def _resident(shape):
    nd = len(shape)
    return pl.BlockSpec(shape, lambda *_: (0,) * nd, pipeline_mode=pl.Buffered(1))


def _params(semantics):
    return pltpu.CompilerParams(dimension_semantics=semantics, vmem_limit_bytes=VMEM_LIMIT)


def _rms(x, gain):
    return x * lax.rsqrt(jnp.mean(x * x, axis=-1, keepdims=True) + EPS) * gain


def _sigmoid(x):
    return 1.0 / (1.0 + jnp.exp(-x))


def _softplus(x):
    return jnp.maximum(x, 0.0) + jnp.log1p(jnp.exp(-jnp.abs(x)))


def _dot(a, b):
    return jnp.dot(a, b, preferred_element_type=F32)


def _dot_nt(a, b):
    return lax.dot_general(a, b, (((1,), (1,)), ((), ())), preferred_element_type=F32)


def _dot_tn(a, b):
    return lax.dot_general(a, b, (((0,), (0,)), ((), ())), preferred_element_type=F32)


def _split3(x):
    pieces = []
    rest = x
    for _ in range(SPLIT_PIECES):
        piece = rest.astype(BF16)
        pieces.append(piece)
        rest = rest - piece.astype(F32)
    return pieces


def _silu(x):
    return x * _sigmoid(x)


def _proj_cols(xn, w_ref, off, width, o_ref, post=None, rows=None):
    for c0 in range(0, width, MATMUL_COLS):
        cw = min(MATMUL_COLS, width - c0)
        acc = _dot(xn, w_ref[:, off + c0:off + c0 + cw])
        if post is not None:
            acc = post(acc)
        if rows is None:
            o_ref[:, c0:c0 + cw] = acc.astype(o_ref.dtype)
        else:
            o_ref[rows, c0:c0 + cw] = acc.astype(o_ref.dtype)


_EVEN_MAIN = (SSD_INNER, SSD_XBC, GLA_KEY, GLA_KEY, GLA_VAL, GLA_VAL)


def _even_in_kernel(x_ref, g_ref, w_ref, ws_ref, w2_ref, gb_ref, cw_ref, cb_ref,
                    zg_ref, u_ref, q_ref, k_ref, v_ref, rg_ref, small_ref, logg_ref, conv_buf, *, tiles_per_seq):
    tm = x_ref.shape[0]

    @pl.when(pl.program_id(0) % tiles_per_seq == 0)
    def _():
        conv_buf[0:SUBLANES, :] = jnp.zeros((SUBLANES, SSD_XBC), F32)

    xn = _rms(x_ref[...], g_ref[...]).astype(BF16)
    offsets = dict(zip("z xbc q k v r".split(), (int(o) for o in np.cumsum((0,) + _EVEN_MAIN)[:-1])))
    widths = dict(zip("z xbc q k v r".split(), _EVEN_MAIN))

    def conv_piece(c0):
        cs = slice(c0, c0 + CONV_COLS)
        acc = cb_ref[:, cs]
        for tap in range(SSD_CONV):
            start = SUBLANES - (SSD_CONV - 1) + tap
            acc = acc + cw_ref[tap:tap + 1, cs] * conv_buf[start:start + tm, cs]
        conv_buf[0:SUBLANES, cs] = conv_buf[tm:tm + SUBLANES, cs]
        u_ref[:, cs] = _silu(acc).astype(u_ref.dtype)

    _proj_cols(xn, w_ref, offsets["xbc"], widths["xbc"], conv_buf, rows=slice(SUBLANES, SUBLANES + tm))
    conv_starts = list(range(0, SSD_XBC, CONV_COLS))
    for name, o_ref, post in (("z", zg_ref, _silu), ("q", q_ref, lambda a: a * GLA_HK ** -0.5), ("k", k_ref, None),
                              ("v", v_ref, None), ("r", rg_ref, _silu)):
        _proj_cols(xn, w_ref, offsets[name], widths[name], o_ref, post)
        if conv_starts:
            conv_piece(conv_starts.pop(0))
    for c0 in conv_starts:
        conv_piece(c0)
    small = _dot(xn, ws_ref[...])
    small_ref[...] = small
    logits = _dot(small.astype(BF16), w2_ref[...]) + gb_ref[...]
    logg_ref[...] = -_softplus(-logits) * (LOG2E / GLA_TAU)


def _even_in_proj(x, gain, w_main, w_small, w2_pad, gla_b, conv_w, conv_b, seq):
    t = x.shape[0]
    tm = min(TOKEN_TILE, seq)
    row = lambda w: pl.BlockSpec((tm, w), lambda i: (i, 0))
    out_shape = [jax.ShapeDtypeStruct((t, w), BF16) for w in _EVEN_MAIN]
    out_shape += [jax.ShapeDtypeStruct((t, LANES), F32), jax.ShapeDtypeStruct((t, GLA_KEY), F32)]
    return pl.pallas_call(
        functools.partial(_even_in_kernel, tiles_per_seq=seq // tm),
        grid=(t // tm,),
        in_specs=[row(D_MODEL), _resident((1, D_MODEL)), _resident(w_main.shape), _resident(w_small.shape),
                  _resident(w2_pad.shape), _resident((1, GLA_KEY)), _resident((SSD_CONV, SSD_XBC)),
                  _resident((1, SSD_XBC))],
        out_specs=[row(w) for w in _EVEN_MAIN] + [row(LANES), row(GLA_KEY)],
        out_shape=out_shape,
        scratch_shapes=[pltpu.VMEM((tm + SUBLANES, SSD_XBC), F32)],
        compiler_params=_params(("arbitrary",)),
        name="even_in_proj",
    )(x, gain, w_main, w_small, w2_pad, gla_b, conv_w, conv_b[None, :])


def _ssd_kernel(u_ref, zg_ref, small_ref, dtb_ref, alog_ref, dskip_ref, nw_ref, expand_ref,
                tri_ref, y_ref, state_ref, y_buf, *, chunks):
    @pl.when(pl.program_id(1) == 0)
    def _():
        state_ref[...] = jnp.zeros_like(state_ref)

    for c in range(chunks):
        rows = pl.ds(c * SSD_CHUNK, SSD_CHUNK)
        _ssd_chunk(u_ref.at[rows], zg_ref.at[rows], small_ref.at[rows], dtb_ref, alog_ref,
                   dskip_ref, nw_ref, expand_ref, tri_ref, y_ref.at[rows], state_ref, y_buf)


def _ssd_chunk(u_ref, zg_ref, small_ref, dtb_ref, alog_ref, dskip_ref, nw_ref, expand_ref,
               tri_ref, y_ref, state_ref, y_buf):
    q = SSD_CHUNK
    u = u_ref[...].astype(F32)
    xs = u[:, :SSD_INNER]
    bm = u[:, SSD_INNER:SSD_INNER + SSD_GROUPS * SSD_STATE]
    cm = u[:, SSD_INNER + SSD_GROUPS * SSD_STATE:]
    bm_t = bm.T

    lane = lax.broadcasted_iota(jnp.int32, (q, LANES), 1)
    dt = jnp.where(lane < SSD_HEADS, _softplus(small_ref[...] + dtb_ref[...]), 0.0)
    loga = dt * (-jnp.exp(alog_ref[...]) * LOG2E)
    cum = _dot(tri_ref[...], jnp.concatenate(_split3(loga), axis=0))
    cum_row = cum.T
    expand = expand_ref[...]
    dt_full = _dot(jnp.concatenate(_split3(dt), axis=1), expand)
    cum_full = _dot(jnp.concatenate(_split3(cum), axis=1), expand)
    last_full = cum_full[q - 1:q, :]
    xdt = xs * dt_full
    xdt_b = xdt.astype(BF16)
    tril = lax.broadcasted_iota(jnp.int32, (q, q), 0) >= lax.broadcasted_iota(jnp.int32, (q, q), 1)

    for g in range(SSD_GROUPS):
        gs = slice(g * SSD_GROUP_WIDTH, (g + 1) * SSD_GROUP_WIDTH)
        cg = cm[:, g * SSD_STATE:(g + 1) * SSD_STATE].astype(BF16)
        bg = bm[:, g * SSD_STATE:(g + 1) * SSD_STATE].astype(BF16)
        scores = _dot_nt(cg, bg)
        st = state_ref[g]
        y_inter = _dot(cg, st.astype(BF16)) * jnp.exp2(cum_full[:, gs])
        for e in range(SSD_GROUP_HEADS):
            h = g * SSD_GROUP_HEADS + e
            hs = slice(h * SSD_HEAD_DIM, (h + 1) * SSD_HEAD_DIM)
            seg = cum[:, h:h + 1] - cum_row[h:h + 1, :]
            decay = jnp.exp2(jnp.where(tril, seg, NEG_BIG))
            ye = _dot((scores * decay).astype(BF16), xdt_b[:, hs])
            y_buf[:, hs] = ye + y_inter[:, e * SSD_HEAD_DIM:(e + 1) * SSD_HEAD_DIM]
        wts = jnp.exp2(last_full[:, gs] - cum_full[:, gs])
        xw = (xdt[:, gs] * wts).astype(BF16)
        bg_t = bm_t[g * SSD_STATE:(g + 1) * SSD_STATE, :].astype(BF16)
        state_ref[g] = st * jnp.exp2(last_full[:, gs]) + _dot(bg_t, xw)

    y = (y_buf[...] + dskip_ref[...] * xs) * zg_ref[...].astype(F32)
    for g in range(SSD_GROUPS):
        gs = slice(g * SSD_GROUP_WIDTH, (g + 1) * SSD_GROUP_WIDTH)
        y_ref[:, gs] = _rms(y[:, gs], nw_ref[:, gs]).astype(y_ref.dtype)


def _ssd_scan(u, zg, small, dt_bias, a_log, d_skip, norm_w, batch, chunks):
    t = u.shape[0]
    q = SSD_CHUNK
    nc = t // batch // (q * chunks)
    pad = lambda v: jnp.pad(v.astype(F32), (0, LANES - v.shape[0]))[None, :]
    expand = (np.arange(LANES)[:, None] == (np.arange(SSD_INNER)[None, :] // SSD_HEAD_DIM)).astype(np.float32)
    expand = np.tile(expand, (SPLIT_PIECES, 1))
    tri = np.tile(np.tril(np.ones((q, q), np.float32)), (1, SPLIT_PIECES))
    row = lambda w: pl.BlockSpec((q * chunks, w), lambda b, c: (b * nc + c, 0))
    return pl.pallas_call(
        functools.partial(_ssd_kernel, chunks=chunks),
        grid=(batch, nc),
        in_specs=[row(SSD_XBC), row(SSD_INNER), row(LANES),
                  _resident((1, LANES)), _resident((1, LANES)), _resident((1, SSD_INNER)), _resident((1, SSD_INNER)),
                  _resident((SPLIT_PIECES * LANES, SSD_INNER)), _resident((q, SPLIT_PIECES * q))],
        out_specs=row(SSD_INNER),
        out_shape=jax.ShapeDtypeStruct((t, SSD_INNER), BF16),
        scratch_shapes=[pltpu.VMEM((SSD_GROUPS, SSD_STATE, SSD_GROUP_WIDTH), F32),
                        pltpu.VMEM((q, SSD_INNER), F32)],
        compiler_params=_params(("arbitrary", "arbitrary")),
        name="ssd_scan",
    )(u, zg, small, pad(dt_bias), pad(a_log),
      jnp.repeat(d_skip.astype(F32), SSD_HEAD_DIM)[None, :], norm_w[None, :], jnp.asarray(expand, BF16),
      jnp.asarray(tri, BF16))


def _gla_kernel(q_ref, k_ref, v_ref, g_ref, r_ref, nw_ref, tri_ref, o_ref,
                state_ref, k_pad, c_pad, v_pad, term_buf, intra_buf, *, chunks):
    @pl.when(pl.program_id(1) == 0)
    def _():
        state_ref[...] = jnp.zeros_like(state_ref)

    def run(fast):
        for c in range(chunks):
            rows = pl.ds(c * GLA_CHUNK, GLA_CHUNK)
            _gla_chunk(q_ref.at[rows], k_ref.at[rows], v_ref.at[rows], g_ref.at[rows], r_ref.at[rows], nw_ref,
                       tri_ref, o_ref.at[rows], state_ref, k_pad, c_pad, v_pad, term_buf, intra_buf, fast)

    decay = -jnp.sum(g_ref[...].reshape(chunks, GLA_CHUNK, GLA_KEY), axis=1)
    safe = jnp.max(decay) <= GLA_SAFE_LOG2
    pl.when(safe)(lambda: run(True))
    pl.when(jnp.logical_not(safe))(lambda: run(False))


def _gla_chunk(q_ref, k_ref, v_ref, g_ref, r_ref, nw_ref, tri_ref, o_ref,
               state_ref, k_pad, c_pad, v_pad, term_buf, intra_buf, fast):
    n = GLA_CHUNK
    sub = GLA_SUB
    nsub = n // sub
    heads = [(slice(h * GLA_HK, (h + 1) * GLA_HK), slice(h * GLA_HV, (h + 1) * GLA_HV)) for h in range(GLA_HEADS)]
    row = lax.broadcasted_iota(jnp.int32, (n, n), 0)
    col = lax.broadcasted_iota(jnp.int32, (n, n), 1)
    cums = [_dot(tri_ref[...], jnp.concatenate(_split3(g_ref[:, ks]), axis=0)) for ks, _ in heads]

    def sub_ref(cum, i):
        return cum[i * sub - 1:i * sub, :] if i > 0 else jnp.zeros((1, GLA_HK), F32)

    def factored_blocks(h, first, clamp):
        ks, _ = heads[h]
        cum = cums[h]
        qh = q_ref[:, ks].astype(F32)
        kh = k_ref[:, ks].astype(F32)
        blocks = [jnp.zeros((sub, n), F32)] * first
        for i in range(first, nsub):
            ref = sub_ref(cum, i)
            qt = qh[i * sub:(i + 1) * sub, :] * jnp.exp2(cum[i * sub:(i + 1) * sub, :] - ref)
            kt = kh * jnp.exp2(jnp.minimum(ref - cum, clamp))
            blocks.append(_dot_nt(qt.astype(BF16), kt.astype(BF16)))
        return jnp.concatenate(blocks, axis=0)

    if not fast:
        k_pad[0:sub, :] = jnp.zeros((sub, GLA_HK), F32)
        c_pad[0:sub, :] = jnp.zeros((sub, GLA_HK), F32)
        v_pad[0:sub, :] = jnp.zeros((sub, GLA_HV), F32)
        ones = jnp.ones((GLA_HK, GLA_HV), BF16)
        for h, (ks, vs) in enumerate(heads):
            cum = cums[h]
            qh = q_ref[:, ks].astype(F32)
            vh_b = v_ref[:, vs]
            a_far = jnp.where((row - col) >= sub, factored_blocks(h, 1, 0.0), 0.0)
            o = _dot(a_far.astype(BF16), vh_b)
            k_pad[sub:sub + n, :] = k_ref[:, ks].astype(F32)
            c_pad[sub:sub + n, :] = cum
            v_pad[sub:sub + n, :] = vh_b.astype(F32)
            for d in range(sub):
                kd = k_pad[sub - d:sub - d + n, :]
                cd = c_pad[sub - d:sub - d + n, :]
                term_buf[d * n:(d + 1) * n, :] = (qh * kd * jnp.exp2(cum - cd)).astype(BF16)
            near = _dot(term_buf[...], ones)
            for d in range(sub):
                o = o + near[d * n:(d + 1) * n, :] * v_pad[sub - d:sub - d + n, :]
            intra_buf[:, vs] = o

    for h, (ks, vs) in enumerate(heads):
        cum = cums[h]
        last = cum[n - 1:n, :]
        qh = q_ref[:, ks].astype(F32)
        kh = k_ref[:, ks].astype(F32)
        vh_b = v_ref[:, vs]
        st = state_ref[h]
        qd = (qh * jnp.exp2(cum)).astype(BF16)
        if fast:
            kd = (kh * jnp.exp2(-cum)).astype(BF16)
            a = jnp.where(row >= col, _dot_nt(qd, kd), 0.0)
            intra = _dot(a.astype(BF16), vh_b)
        else:
            intra = intra_buf[:, vs]
        o = intra + _dot_nt(qd, st.astype(BF16))
        state_ref[h] = st * jnp.exp2(last) + _dot_tn(vh_b, (kh * jnp.exp2(last - cum)).astype(BF16))
        o_ref[:, vs] = (_rms(o, nw_ref[...]) * r_ref[:, vs].astype(F32)).astype(o_ref.dtype)


def _gla_scan(q, k, v, logg, r, norm_w, batch, chunks):
    t = q.shape[0]
    n = GLA_CHUNK
    nc = t // batch // (n * chunks)
    tri = np.tile(np.tril(np.ones((n, n), np.float32)), (1, SPLIT_PIECES))
    row = lambda w: pl.BlockSpec((n * chunks, w), lambda b, c: (b * nc + c, 0))
    return pl.pallas_call(
        functools.partial(_gla_kernel, chunks=chunks),
        grid=(batch, nc),
        in_specs=[row(GLA_KEY), row(GLA_KEY), row(GLA_VAL), row(GLA_KEY), row(GLA_VAL),
                  _resident((1, GLA_HV)), _resident((n, SPLIT_PIECES * n))],
        out_specs=row(GLA_VAL),
        out_shape=jax.ShapeDtypeStruct((t, GLA_VAL), BF16),
        scratch_shapes=[pltpu.VMEM((GLA_HEADS, GLA_HV, GLA_HK), F32),
                        pltpu.VMEM((n + GLA_SUB, GLA_HK), F32),
                        pltpu.VMEM((n + GLA_SUB, GLA_HK), F32),
                        pltpu.VMEM((n + GLA_SUB, GLA_HV), F32),
                        pltpu.VMEM((GLA_SUB * n, GLA_HK), BF16),
                        pltpu.VMEM((n, GLA_VAL), F32)],
        compiler_params=_params(("arbitrary", "arbitrary")),
        name="gla_scan",
    )(q, k, v, logg, r, norm_w[None, :], jnp.asarray(tri, BF16))


def _proj_residual_kernel(*refs, widths):
    x_ref = refs[0]
    a_refs = refs[1:1 + len(widths)]
    w_ref = refs[1 + len(widths)]
    o_ref = refs[2 + len(widths)]
    for c0 in range(0, D_MODEL, MATMUL_COLS):
        cs = slice(c0, c0 + MATMUL_COLS)
        acc = x_ref[:, cs]
        off = 0
        for a_ref, width in zip(a_refs, widths):
            acc = acc + _dot(a_ref[...], w_ref[off:off + width, cs])
            off += width
        o_ref[:, cs] = acc


def _proj_residual(x, acts, w, tile):
    t = x.shape[0]
    tm = min(tile, t)
    widths = tuple(a.shape[1] for a in acts)
    row = lambda wd: pl.BlockSpec((tm, wd), lambda i: (i, 0))
    return pl.pallas_call(
        functools.partial(_proj_residual_kernel, widths=widths),
        grid=(t // tm,),
        in_specs=[row(D_MODEL)] + [row(wd) for wd in widths] + [_resident(w.shape)],
        out_specs=row(D_MODEL),
        out_shape=jax.ShapeDtypeStruct((t, D_MODEL), F32),
        compiler_params=_params(("parallel",)),
        name="proj_residual",
    )(x, *acts, w)


def _norm_proj_kernel(x_ref, g_ref, w_ref, *o_refs):
    xn = _rms(x_ref[...], g_ref[...]).astype(BF16)
    off = 0
    for o_ref in o_refs:
        width = o_ref.shape[1]
        _proj_cols(xn, w_ref, off, width, o_ref)
        off += width


def _norm_proj(x, gain, w, widths):
    t = x.shape[0]
    tm = min(TOKEN_TILE, t)
    row = lambda wd: pl.BlockSpec((tm, wd), lambda i: (i, 0))
    return pl.pallas_call(
        _norm_proj_kernel,
        grid=(t // tm,),
        in_specs=[row(D_MODEL), _resident((1, D_MODEL)), _resident(w.shape)],
        out_specs=[row(wd) for wd in widths],
        out_shape=[jax.ShapeDtypeStruct((t, wd), BF16) for wd in widths],
        compiler_params=_params(("parallel",)),
        name="norm_proj",
    )(x, gain, w)


def _qkv_proj_kernel(x_ref, g_ref, wqt_ref, wk_ref, wvt_ref, qt_ref, k_ref, vt_ref):
    xn = _rms(x_ref[...], g_ref[...]).astype(BF16)
    for c0 in range(0, DIFF_QK, MATMUL_COLS):
        cs = slice(c0, c0 + MATMUL_COLS)
        qt_ref[cs, :] = (_dot_nt(wqt_ref[cs, :], xn) * DIFF_Q_SCALE).astype(BF16)
        vt_ref[cs, :] = _dot_nt(wvt_ref[cs, :], xn).astype(BF16)
    _proj_cols(xn, wk_ref, 0, DIFF_QK, k_ref)


def _qkv_proj(x, gain, wq_t, wk, wv_t, tm):
    t = x.shape[0]
    row = lambda wd: pl.BlockSpec((tm, wd), lambda i: (i, 0))
    tr = pl.BlockSpec((None, DIFF_QK, tm), lambda i: (i, 0, 0))
    tr_shape = jax.ShapeDtypeStruct((t // tm, DIFF_QK, tm), BF16)
    return pl.pallas_call(
        _qkv_proj_kernel,
        grid=(t // tm,),
        in_specs=[row(D_MODEL), _resident((1, D_MODEL)), _resident(wq_t.shape), _resident(wk.shape),
                  _resident(wv_t.shape)],
        out_specs=[tr, row(DIFF_QK), tr],
        out_shape=[tr_shape, jax.ShapeDtypeStruct((t, DIFF_QK), BF16), tr_shape],
        compiler_params=_params(("parallel",)),
        name="qkv_proj",
    )(x, gain, wq_t, wk, wv_t)


def _diff_attn_kernel(slope_ref, qt_ref, k_ref, kx_ref, vt_ref, lam_ref, nw_ref, o_ref,
                      acc_ref, m_ref, qa_ref, s_a, s_b, mx_a, mx_b, *, lambda_init):
    tile = qt_ref.shape[1]
    h = pl.program_id(1)
    i = pl.program_id(2)
    tile_step = slope_ref[h] * (LOG2E * tile)
    half = DIFF_HEAD_DIM

    qt = qt_ref[...]
    zeros = jnp.zeros((half, tile), BF16)
    ones_rows = (lax.broadcasted_iota(jnp.int32, (LANES, tile), 0) < ALIBI_PIECES).astype(BF16)
    qa_ref[:, :tile] = jnp.concatenate([qt[:half], zeros, ones_rows], axis=0)
    qa_ref[:, tile:] = jnp.concatenate([zeros, qt[half:], ones_rows], axis=0)
    acc_ref[...] = jnp.zeros_like(acc_ref)
    m_ref[...] = jnp.full_like(m_ref, NEG_BIG)

    def scores(t, buf, masked):
        s_ref, mx_ref = buf
        start = pl.multiple_of(t * tile, tile)
        k_aug = jnp.concatenate([k_ref[pl.ds(start, tile), :], kx_ref[...]], axis=1)
        s = _dot(k_aug, qa_ref[...])
        if masked:
            key_pos = lax.broadcasted_iota(jnp.int32, (tile, 2 * tile), 0)
            lane_pos = lax.broadcasted_iota(jnp.int32, (tile, 2 * tile), 1)
            causal = key_pos <= jnp.where(lane_pos >= tile, lane_pos - tile, lane_pos)
            s = jnp.where(causal, s, NEG_BIG)
        s_ref[...] = s
        mx_ref[...] = jnp.max(s.reshape(tile // SUBLANES, SUBLANES, 2 * tile), axis=0)

    def softmax_pv(t, buf):
        s_ref, mx_ref = buf
        off = -tile_step * (i - t).astype(F32)
        m_old = m_ref[0:1, :]
        m_new = jnp.maximum(m_old, jnp.max(mx_ref[...], axis=0, keepdims=True) + off)
        alpha = jnp.exp2(m_old - m_new)
        shift = off - m_new
        v_aug = jnp.concatenate([vt_ref[t], jnp.ones((DENOM_ROWS, tile), BF16)], axis=0)
        p = jnp.exp2(s_ref[...] + shift)
        acc_ref[...] = alpha * acc_ref[...] + _dot(v_aug, p.astype(BF16))
        m_ref[0:1, :] = m_new

    buf_a = (s_a, mx_a)
    buf_b = (s_b, mx_b)
    even = i % 2 == 0

    @pl.when(i == 0)
    def _():
        scores(0, buf_a, True)

    @pl.when(jnp.logical_and(even, i > 0))
    def _():
        scores(0, buf_a, False)
        scores(1, buf_b, False)
        softmax_pv(0, buf_a)

    @pl.when(jnp.logical_not(even))
    def _():
        scores(0, buf_b, False)

    first = jnp.where(even, 1, 0)

    def pair(u, carry):
        t = first + 2 * u
        scores(t + 1, buf_a, False)
        softmax_pv(t, buf_b)
        scores(t + 2, buf_b, False)
        softmax_pv(t + 1, buf_a)
        return carry

    lax.fori_loop(0, (i - 1) // 2, pair, 0)

    @pl.when(i > 0)
    def _():
        scores(i, buf_a, True)
        softmax_pv(i - 1, buf_b)

    softmax_pv(i, buf_a)

    lam = (jnp.exp(jnp.sum(lam_ref[0:1, :] * lam_ref[1:2, :], axis=-1, keepdims=True))
           - jnp.exp(jnp.sum(lam_ref[2:3, :] * lam_ref[3:4, :], axis=-1, keepdims=True)) + lambda_init)
    normed = acc_ref[0:DIFF_V_DIM, :] / acc_ref[DIFF_V_DIM:DIFF_V_DIM + 1, :]
    o = (normed[:, :tile] - lam * normed[:, tile:]).T
    o_ref[...] = (_rms(o, nw_ref[...]) * (1.0 - lambda_init)).astype(o_ref.dtype)


def _alibi_slopes(n):
    start = 2.0 ** (-8.0 / n)
    return np.array([start ** (i + 1) for i in range(n)], dtype=np.float32)


def _alibi_key_columns(tile):
    target = (LOG2E * _alibi_slopes(DIFF_HEADS))[:, None] * np.arange(tile, dtype=np.float32)[None, :]
    rest = jnp.asarray(target, F32)
    pieces = []
    for _ in range(ALIBI_PIECES):
        piece = rest.astype(BF16)
        pieces.append(piece)
        rest = rest - piece.astype(F32)
    cols = jnp.stack(pieces, axis=-1)
    return jnp.pad(cols, ((0, 0), (0, 0), (0, LANES - ALIBI_PIECES)))


def _diff_attention(qt, k, vt, lam_vecs, subln, lambda_init, batch):
    nblk, _, tile = qt.shape
    nq = nblk // batch
    seq = nq * tile
    return pl.pallas_call(
        functools.partial(_diff_attn_kernel, lambda_init=lambda_init),
        grid=(batch, DIFF_HEADS, nq),
        in_specs=[pl.BlockSpec(memory_space=pltpu.SMEM),
                  pl.BlockSpec((None, LANES, tile), lambda b, h, i: (b * nq + i, h, 0)),
                  pl.BlockSpec((seq, LANES), lambda b, h, i: (b, h)),
                  pl.BlockSpec((None, tile, LANES), lambda b, h, i: (h, 0, 0)),
                  pl.BlockSpec((nq, LANES, tile), lambda b, h, i: (b, h, 0)),
                  _resident((4, DIFF_HEAD_DIM)), _resident((1, DIFF_V_DIM))],
        out_specs=pl.BlockSpec((tile, LANES), lambda b, h, i: (b * nq + i, h)),
        out_shape=jax.ShapeDtypeStruct((batch * seq, DIFF_HEADS * DIFF_V_DIM), BF16),
        scratch_shapes=[pltpu.VMEM((DIFF_V_DIM + DENOM_ROWS, 2 * tile), F32), pltpu.VMEM((SUBLANES, 2 * tile), F32),
                        pltpu.VMEM((2 * LANES, 2 * tile), BF16),
                        pltpu.VMEM((tile, 2 * tile), F32), pltpu.VMEM((tile, 2 * tile), F32),
                        pltpu.VMEM((SUBLANES, 2 * tile), F32), pltpu.VMEM((SUBLANES, 2 * tile), F32)],
        compiler_params=_params(("arbitrary", "arbitrary", "arbitrary")),
        name="diff_attention",
    )(jnp.asarray(_alibi_slopes(DIFF_HEADS)), qt, k, _alibi_key_columns(tile), vt, lam_vecs, subln[None, :])


def _cross_attn_kernel(x_ref, g_ref, wq_ref, k_ref, v_ref, wo_ref, o_ref, q_buf, att_buf):
    heads = [slice(h * X_HEAD_DIM, (h + 1) * X_HEAD_DIM) for h in range(X_HEADS)]
    xn = _rms(x_ref[...], g_ref[...]).astype(BF16)
    for c0 in range(0, D_MODEL, MATMUL_COLS):
        cs = slice(c0, c0 + MATMUL_COLS)
        q_buf[:, cs] = _dot(xn, wq_ref[:, cs]).astype(BF16)
    scores = [_dot_nt(q_buf[:, hs], k_ref[:, hs]) * (X_HEAD_DIM ** -0.5) for hs in heads]
    for hs, s in zip(heads, scores):
        p = jnp.exp(s - jnp.max(s, axis=-1, keepdims=True))
        pv = _dot(p.astype(BF16), v_ref[:, hs])
        att_buf[:, hs] = (pv / jnp.sum(p, axis=-1, keepdims=True)).astype(BF16)
    for c0 in range(0, D_MODEL, MATMUL_COLS):
        cs = slice(c0, c0 + MATMUL_COLS)
        o_ref[:, cs] = x_ref[:, cs] + _dot(att_buf[...], wo_ref[:, cs])


def _cross_attention(x, gain, wq, k_mem, v_mem, wo, batch):
    t = x.shape[0]
    seq = t // batch
    tm = min(CROSS_TILE, seq)
    per = seq // tm
    n_mem = k_mem.shape[0] // batch
    row = pl.BlockSpec((tm, D_MODEL), lambda i: (i, 0))
    mem = pl.BlockSpec((n_mem, D_MODEL), lambda i: (i // per, 0))
    return pl.pallas_call(
        _cross_attn_kernel,
        grid=(t // tm,),
        in_specs=[row, _resident((1, D_MODEL)), _resident(wq.shape), mem, mem, _resident(wo.shape)],
        out_specs=row,
        out_shape=jax.ShapeDtypeStruct((t, D_MODEL), F32),
        scratch_shapes=[pltpu.VMEM((tm, D_MODEL), BF16), pltpu.VMEM((tm, D_MODEL), BF16)],
        compiler_params=_params(("parallel",)),
        name="cross_attention",
    )(x, gain, wq, k_mem, v_mem, wo)


def _mlp_kernel(x_ref, g_ref, w1_ref, w2_ref, fg_ref, o_ref, acc_ref, *, final_norm):
    x = x_ref[...]
    xn = _rms(x, g_ref[...]).astype(BF16)
    acc_ref[...] = x
    for c0 in range(0, D_FF, MATMUL_COLS):
        hid = jnp.maximum(_dot(xn, w1_ref[:, c0:c0 + MATMUL_COLS]), 0.0)
        acc_ref[...] += _dot((hid * hid).astype(BF16), w2_ref[c0:c0 + MATMUL_COLS, :])
    out = acc_ref[...]
    if final_norm:
        out = _rms(out, fg_ref[...])
    o_ref[...] = out


def _mlp(x, gain, w1, w2, final_gain, final_norm, tile):
    t = x.shape[0]
    tm = min(tile, t)
    row = pl.BlockSpec((tm, D_MODEL), lambda i: (i, 0))
    return pl.pallas_call(
        functools.partial(_mlp_kernel, final_norm=final_norm),
        grid=(t // tm,),
        in_specs=[row, _resident((1, D_MODEL)), _resident(w1.shape), _resident(w2.shape), _resident((1, D_MODEL))],
        out_specs=row,
        out_shape=jax.ShapeDtypeStruct((t, D_MODEL), F32),
        scratch_shapes=[pltpu.VMEM((tm, D_MODEL), F32)],
        compiler_params=_params(("parallel",)),
        name="mlp",
    )(x, gain, w1, w2, final_gain)


def _even_weights(w_in, gla_w2):
    z_end = SSD_INNER
    xbc_end = z_end + SSD_XBC
    dt_end = xbc_end + SSD_HEADS
    v_end = dt_end + 2 * GLA_KEY + GLA_VAL
    glr_end = v_end + GLA_RANK
    w_main = jnp.concatenate([w_in[:, :xbc_end], w_in[:, dt_end:v_end], w_in[:, glr_end:]], axis=1).astype(BF16)
    w_small = jnp.concatenate([w_in[:, xbc_end:dt_end], w_in[:, v_end:glr_end],
                               jnp.zeros((D_MODEL, LANES - SSD_HEADS - GLA_RANK), w_in.dtype)], axis=1).astype(BF16)
    w2_pad = jnp.zeros((LANES, GLA_KEY), F32).at[SSD_HEADS:SSD_HEADS + GLA_RANK].set(gla_w2).astype(BF16)
    return w_main, w_small, w2_pad


def kernel(x, mem, ev_norm, ev_w_in, ev_conv_w, ev_conv_b, ev_dt_bias, ev_a_log, ev_d_skip, ev_ssd_norm, ev_gla_w2, ev_gla_b, ev_gla_norm, ev_w_out, od_norm, od_w_qkv, od_lam_q1, od_lam_k1, od_lam_q2, od_lam_k2, od_subln, od_w_o, xa_norm, xa_mem_norm, xa_wq, xa_wkv, xa_wo, mlp_norm, mlp_w1, mlp_w2, final_norm):
    batch, seq, d = x.shape
    n_mem = mem.shape[1]
    n_layers = xa_norm.shape[0]
    xf = x.reshape(batch * seq, d)
    memf = mem.reshape(batch * n_mem, d)
    for layer in range(n_layers):
        i = layer // 2
        tail_tile = 2 * TOKEN_TILE
        if layer % 2 == 0:
            w_main, w_small, w2_pad = _even_weights(ev_w_in[i], ev_gla_w2[i])
            zg, u, q, k, v, rg, small, logg = _even_in_proj(xf, ev_norm[i][None, :], w_main, w_small, w2_pad,
                                                            ev_gla_b[i][None, :], ev_conv_w[i], ev_conv_b[i], seq)
            y = _ssd_scan(u, zg, small, ev_dt_bias[i], ev_a_log[i], ev_d_skip[i], ev_ssd_norm[i], batch, 1)
            o = _gla_scan(q, k, v, logg, rg, ev_gla_norm[i], batch, GLA_CHUNKS_PER_STEP)
            xf = _proj_residual(xf, [y, o], ev_w_out[i].astype(BF16), tail_tile)
        else:
            lambda_init = 0.8 - 0.6 * math.exp(-0.3 * layer)
            w_qkv = od_w_qkv[i].astype(BF16)
            qt, k, vt = _qkv_proj(xf, od_norm[i][None, :], w_qkv[:, :DIFF_QK].T, w_qkv[:, DIFF_QK:2 * DIFF_QK],
                                  w_qkv[:, 2 * DIFF_QK:].T, min(ATTN_TILE, seq))
            lam_vecs = jnp.stack([od_lam_q1[i], od_lam_k1[i], od_lam_q2[i], od_lam_k2[i]]).astype(F32)
            att = _diff_attention(qt, k, vt, lam_vecs, od_subln[i], lambda_init, batch)
            xf = _proj_residual(xf, [att], od_w_o[i].astype(BF16), tail_tile)
        k_mem, v_mem = _norm_proj(memf, xa_mem_norm[layer][None, :], xa_wkv[layer].astype(BF16), (d, d))
        xf = _cross_attention(xf, xa_norm[layer][None, :], xa_wq[layer].astype(BF16), k_mem, v_mem,
                              xa_wo[layer].astype(BF16), batch)
        xf = _mlp(xf, mlp_norm[layer][None, :], mlp_w1[layer].astype(BF16), mlp_w2[layer].astype(BF16),
                  final_norm[None, :], layer == n_layers - 1, tail_tile)
    return xf.reshape(batch, seq, d)
```

```python
import functools
import math

import numpy as np
import jax
import jax.numpy as jnp
from jax import lax
from jax.experimental import pallas as pl
from jax.experimental.pallas import tpu as pltpu

F32 = jnp.float32
BF16 = jnp.bfloat16
EPS = 1e-5
NEG_BIG = -1e30

D_MODEL = 1024
N_LAYERS = 4

SSD_HEADS = 16
SSD_HEAD_DIM = 64
SSD_INNER = SSD_HEADS * SSD_HEAD_DIM
SSD_GROUPS = 2
SSD_STATE = 64
SSD_CONV = 4
SSD_CHUNK = 128
SSD_CHUNKS_PER_STEP = 4
SSD_XBC = SSD_INNER + 2 * SSD_GROUPS * SSD_STATE
SSD_GROUP_WIDTH = SSD_INNER // SSD_GROUPS
SSD_GROUP_HEADS = SSD_HEADS // SSD_GROUPS

GLA_HEADS = 4
GLA_KEY = 512
GLA_VAL = 1024
GLA_HK = GLA_KEY // GLA_HEADS
GLA_HV = GLA_VAL // GLA_HEADS
GLA_RANK = 16
GLA_TAU = 16.0
GLA_CHUNK = 64
GLA_CHUNKS_PER_STEP = 8
GLA_SUB = 16
GLA_SAFE_LOG2 = 100.0

DIFF_HEADS = 8
DIFF_HEAD_DIM = 64
DIFF_V_DIM = 2 * DIFF_HEAD_DIM
DIFF_QK = DIFF_HEADS * 2 * DIFF_HEAD_DIM
LOG2E = 1.4426950408889634
DIFF_Q_SCALE = DIFF_HEAD_DIM ** -0.5 * LOG2E
ALIBI_PIECES = 3
DENOM_ROWS = 16
SPLIT_PIECES = 3

X_HEADS = 4
X_HEAD_DIM = D_MODEL // X_HEADS
D_FF = 4 * D_MODEL

LANES = 128
SUBLANES = 8
VMEM_LIMIT = 56 * 1024 * 1024
TOKEN_TILE = 512
ATTN_TILE = 1024
CROSS_TILE = 1024
MATMUL_COLS = 512
CONV_COLS = 256


def _resident(shape):
    nd = len(shape)
    return pl.BlockSpec(shape, lambda *_: (0,) * nd, pipeline_mode=pl.Buffered(1))


def _params(semantics):
    return pltpu.CompilerParams(dimension_semantics=semantics, vmem_limit_bytes=VMEM_LIMIT)


def _rms(x, gain):
    return x * lax.rsqrt(jnp.mean(x * x, axis=-1, keepdims=True) + EPS) * gain


def _sigmoid(x):
    return 1.0 / (1.0 + jnp.exp(-x))


def _softplus(x):
    return jnp.maximum(x, 0.0) + jnp.log1p(jnp.exp(-jnp.abs(x)))


def _dot(a, b):
    return jnp.dot(a, b, preferred_element_type=F32)


def _dot_nt(a, b):
    return lax.dot_general(a, b, (((1,), (1,)), ((), ())), preferred_element_type=F32)


def _dot_tn(a, b):
    return lax.dot_general(a, b, (((0,), (0,)), ((), ())), preferred_element_type=F32)


def _split3(x):
    pieces = []
    rest = x
    for _ in range(SPLIT_PIECES):
        piece = rest.astype(BF16)
        pieces.append(piece)
        rest = rest - piece.astype(F32)
    return pieces


def _silu(x):
    return x * _sigmoid(x)


def _proj_cols(xn, w_ref, off, width, o_ref, post=None, rows=None):
    for c0 in range(0, width, MATMUL_COLS):
        cw = min(MATMUL_COLS, width - c0)
        acc = _dot(xn, w_ref[:, off + c0:off + c0 + cw])
        if post is not None:
            acc = post(acc)
        if rows is None:
            o_ref[:, c0:c0 + cw] = acc.astype(o_ref.dtype)
        else:
            o_ref[rows, c0:c0 + cw] = acc.astype(o_ref.dtype)


_EVEN_MAIN = (SSD_INNER, SSD_XBC, GLA_KEY, GLA_KEY, GLA_VAL, GLA_VAL)


def _even_in_kernel(x_ref, g_ref, w_ref, ws_ref, w2_ref, gb_ref, cw_ref, cb_ref,
                    zg_ref, u_ref, q_ref, k_ref, v_ref, rg_ref, small_ref, logg_ref, conv_buf, *, tiles_per_seq):
    tm = x_ref.shape[0]

    @pl.when(pl.program_id(0) % tiles_per_seq == 0)
    def _():
        conv_buf[0:SUBLANES, :] = jnp.zeros((SUBLANES, SSD_XBC), F32)

    xn = _rms(x_ref[...], g_ref[...]).astype(BF16)
    offsets = dict(zip("z xbc q k v r".split(), (int(o) for o in np.cumsum((0,) + _EVEN_MAIN)[:-1])))
    widths = dict(zip("z xbc q k v r".split(), _EVEN_MAIN))

    def conv_piece(c0):
        cs = slice(c0, c0 + CONV_COLS)
        acc = cb_ref[:, cs]
        for tap in range(SSD_CONV):
            start = SUBLANES - (SSD_CONV - 1) + tap
            acc = acc + cw_ref[tap:tap + 1, cs] * conv_buf[start:start + tm, cs]
        conv_buf[0:SUBLANES, cs] = conv_buf[tm:tm + SUBLANES, cs]
        u_ref[:, cs] = _silu(acc).astype(u_ref.dtype)

    _proj_cols(xn, w_ref, offsets["xbc"], widths["xbc"], conv_buf, rows=slice(SUBLANES, SUBLANES + tm))
    conv_starts = list(range(0, SSD_XBC, CONV_COLS))
    for name, o_ref, post in (("z", zg_ref, _silu), ("q", q_ref, lambda a: a * GLA_HK ** -0.5), ("k", k_ref, None),
                              ("v", v_ref, None), ("r", rg_ref, _silu)):
        _proj_cols(xn, w_ref, offsets[name], widths[name], o_ref, post)
        if conv_starts:
            conv_piece(conv_starts.pop(0))
    for c0 in conv_starts:
        conv_piece(c0)
    small = _dot(xn, ws_ref[...])
    small_ref[...] = small
    logits = _dot(small.astype(BF16), w2_ref[...]) + gb_ref[...]
    logg_ref[...] = -_softplus(-logits) * (LOG2E / GLA_TAU)


def _even_in_proj(x, gain, w_main, w_small, w2_pad, gla_b, conv_w, conv_b, seq):
    t = x.shape[0]
    tm = min(TOKEN_TILE, seq)
    row = lambda w: pl.BlockSpec((tm, w), lambda i: (i, 0))
    out_shape = [jax.ShapeDtypeStruct((t, w), BF16) for w in _EVEN_MAIN]
    out_shape += [jax.ShapeDtypeStruct((t, LANES), F32), jax.ShapeDtypeStruct((t, GLA_KEY), F32)]
    return pl.pallas_call(
        functools.partial(_even_in_kernel, tiles_per_seq=seq // tm),
        grid=(t // tm,),
        in_specs=[row(D_MODEL), _resident((1, D_MODEL)), _resident(w_main.shape), _resident(w_small.shape),
                  _resident(w2_pad.shape), _resident((1, GLA_KEY)), _resident((SSD_CONV, SSD_XBC)),
                  _resident((1, SSD_XBC))],
        out_specs=[row(w) for w in _EVEN_MAIN] + [row(LANES), row(GLA_KEY)],
        out_shape=out_shape,
        scratch_shapes=[pltpu.VMEM((tm + SUBLANES, SSD_XBC), F32)],
        compiler_params=_params(("arbitrary",)),
        name="even_in_proj",
    )(x, gain, w_main, w_small, w2_pad, gla_b, conv_w, conv_b[None, :])


def _ssd_kernel(u_ref, zg_ref, small_ref, dtb_ref, alog_ref, dskip_ref, nw_ref, expand_ref,
                tri_ref, y_ref, state_ref, y_buf, *, chunks):
    @pl.when(pl.program_id(1) == 0)
    def _():
        state_ref[...] = jnp.zeros_like(state_ref)

    for c in range(chunks):
        rows = pl.ds(c * SSD_CHUNK, SSD_CHUNK)
        _ssd_chunk(u_ref.at[rows], zg_ref.at[rows], small_ref.at[rows], dtb_ref, alog_ref,
                   dskip_ref, nw_ref, expand_ref, tri_ref, y_ref.at[rows], state_ref, y_buf.at[rows])


def _ssd_chunk(u_ref, zg_ref, small_ref, dtb_ref, alog_ref, dskip_ref, nw_ref, expand_ref,
               tri_ref, y_ref, state_ref, y_buf):
    q = SSD_CHUNK
    u = u_ref[...].astype(F32)
    xs = u[:, :SSD_INNER]
    bm = u[:, SSD_INNER:SSD_INNER + SSD_GROUPS * SSD_STATE]
    cm = u[:, SSD_INNER + SSD_GROUPS * SSD_STATE:]
    bm_t = bm.T

    lane = lax.broadcasted_iota(jnp.int32, (q, LANES), 1)
    dt = jnp.where(lane < SSD_HEADS, _softplus(small_ref[...] + dtb_ref[...]), 0.0)
    loga = dt * (-jnp.exp(alog_ref[...]) * LOG2E)
    cum = _dot(tri_ref[...], jnp.concatenate(_split3(loga), axis=0))
    cum_row = cum.T
    expand = expand_ref[...]
    dt_full = _dot(jnp.concatenate(_split3(dt), axis=1), expand)
    cum_full = _dot(jnp.concatenate(_split3(cum), axis=1), expand)
    last_full = cum_full[q - 1:q, :]
    xdt = xs * dt_full
    xdt_b = xdt.astype(BF16)
    tril = lax.broadcasted_iota(jnp.int32, (q, q), 0) >= lax.broadcasted_iota(jnp.int32, (q, q), 1)

    for g in range(SSD_GROUPS):
        gs = slice(g * SSD_GROUP_WIDTH, (g + 1) * SSD_GROUP_WIDTH)
        cg = cm[:, g * SSD_STATE:(g + 1) * SSD_STATE].astype(BF16)
        bg = bm[:, g * SSD_STATE:(g + 1) * SSD_STATE].astype(BF16)
        scores = _dot_nt(cg, bg)
        st = state_ref[g]
        y_inter = _dot(cg, st.astype(BF16)) * jnp.exp2(cum_full[:, gs])
        for e in range(SSD_GROUP_HEADS):
            h = g * SSD_GROUP_HEADS + e
            hs = slice(h * SSD_HEAD_DIM, (h + 1) * SSD_HEAD_DIM)
            seg = cum[:, h:h + 1] - cum_row[h:h + 1, :]
            decay = jnp.exp2(jnp.where(tril, seg, NEG_BIG))
            ye = _dot((scores * decay).astype(BF16), xdt_b[:, hs])
            y_buf[:, hs] = ye + y_inter[:, e * SSD_HEAD_DIM:(e + 1) * SSD_HEAD_DIM]
        wts = jnp.exp2(last_full[:, gs] - cum_full[:, gs])
        xw = (xdt[:, gs] * wts).astype(BF16)
        bg_t = bm_t[g * SSD_STATE:(g + 1) * SSD_STATE, :].astype(BF16)
        state_ref[g] = st * jnp.exp2(last_full[:, gs]) + _dot(bg_t, xw)

    y = (y_buf[...] + dskip_ref[...] * xs) * zg_ref[...].astype(F32)
    for g in range(SSD_GROUPS):
        gs = slice(g * SSD_GROUP_WIDTH, (g + 1) * SSD_GROUP_WIDTH)
        y_ref[:, gs] = _rms(y[:, gs], nw_ref[:, gs]).astype(y_ref.dtype)


def _ssd_scan(u, zg, small, dt_bias, a_log, d_skip, norm_w, batch, chunks):
    t = u.shape[0]
    q = SSD_CHUNK
    nc = t // batch // (q * chunks)
    pad = lambda v: jnp.pad(v.astype(F32), (0, LANES - v.shape[0]))[None, :]
    expand = (np.arange(LANES)[:, None] == (np.arange(SSD_INNER)[None, :] // SSD_HEAD_DIM)).astype(np.float32)
    expand = np.tile(expand, (SPLIT_PIECES, 1))
    tri = np.tile(np.tril(np.ones((q, q), np.float32)), (1, SPLIT_PIECES))
    row = lambda w: pl.BlockSpec((q * chunks, w), lambda b, c: (b * nc + c, 0))
    return pl.pallas_call(
        functools.partial(_ssd_kernel, chunks=chunks),
        grid=(batch, nc),
        in_specs=[row(SSD_XBC), row(SSD_INNER), row(LANES),
                  _resident((1, LANES)), _resident((1, LANES)), _resident((1, SSD_INNER)), _resident((1, SSD_INNER)),
                  _resident((SPLIT_PIECES * LANES, SSD_INNER)), _resident((q, SPLIT_PIECES * q))],
        out_specs=row(SSD_INNER),
        out_shape=jax.ShapeDtypeStruct((t, SSD_INNER), BF16),
        scratch_shapes=[pltpu.VMEM((SSD_GROUPS, SSD_STATE, SSD_GROUP_WIDTH), F32),
                        pltpu.VMEM((q * chunks, SSD_INNER), F32)],
        compiler_params=_params(("arbitrary", "arbitrary")),
        name="ssd_scan",
    )(u, zg, small, pad(dt_bias), pad(a_log),
      jnp.repeat(d_skip.astype(F32), SSD_HEAD_DIM)[None, :], norm_w[None, :], jnp.asarray(expand, BF16),
      jnp.asarray(tri, BF16))


def _gla_kernel(q_ref, k_ref, v_ref, g_ref, r_ref, nw_ref, tri_ref, o_ref,
                state_ref, k_pad, c_pad, v_pad, term_buf, intra_buf, *, chunks):
    @pl.when(pl.program_id(1) == 0)
    def _():
        state_ref[...] = jnp.zeros_like(state_ref)

    def run(fast):
        for c in range(chunks):
            rows = pl.ds(c * GLA_CHUNK, GLA_CHUNK)
            _gla_chunk(q_ref.at[rows], k_ref.at[rows], v_ref.at[rows], g_ref.at[rows], r_ref.at[rows], nw_ref,
                       tri_ref, o_ref.at[rows], state_ref, k_pad, c_pad, v_pad, term_buf, intra_buf, fast)

    decay = -jnp.sum(g_ref[...].reshape(chunks, GLA_CHUNK, GLA_KEY), axis=1)
    safe = jnp.max(decay) <= GLA_SAFE_LOG2
    pl.when(safe)(lambda: run(True))
    pl.when(jnp.logical_not(safe))(lambda: run(False))


def _gla_chunk(q_ref, k_ref, v_ref, g_ref, r_ref, nw_ref, tri_ref, o_ref,
               state_ref, k_pad, c_pad, v_pad, term_buf, intra_buf, fast):
    n = GLA_CHUNK
    sub = GLA_SUB
    nsub = n // sub
    heads = [(slice(h * GLA_HK, (h + 1) * GLA_HK), slice(h * GLA_HV, (h + 1) * GLA_HV)) for h in range(GLA_HEADS)]
    row = lax.broadcasted_iota(jnp.int32, (n, n), 0)
    col = lax.broadcasted_iota(jnp.int32, (n, n), 1)
    cums = [_dot(tri_ref[...], jnp.concatenate(_split3(g_ref[:, ks]), axis=0)) for ks, _ in heads]

    def sub_ref(cum, i):
        return cum[i * sub - 1:i * sub, :] if i > 0 else jnp.zeros((1, GLA_HK), F32)

    def factored_blocks(h, first, clamp):
        ks, _ = heads[h]
        cum = cums[h]
        qh = q_ref[:, ks].astype(F32)
        kh = k_ref[:, ks].astype(F32)
        blocks = [jnp.zeros((sub, n), F32)] * first
        for i in range(first, nsub):
            ref = sub_ref(cum, i)
            qt = qh[i * sub:(i + 1) * sub, :] * jnp.exp2(cum[i * sub:(i + 1) * sub, :] - ref)
            kt = kh * jnp.exp2(jnp.minimum(ref - cum, clamp))
            blocks.append(_dot_nt(qt.astype(BF16), kt.astype(BF16)))
        return jnp.concatenate(blocks, axis=0)

    if not fast:
        k_pad[0:sub, :] = jnp.zeros((sub, GLA_HK), F32)
        c_pad[0:sub, :] = jnp.zeros((sub, GLA_HK), F32)
        v_pad[0:sub, :] = jnp.zeros((sub, GLA_HV), F32)
        ones = jnp.ones((GLA_HK, GLA_HV), BF16)
        for h, (ks, vs) in enumerate(heads):
            cum = cums[h]
            qh = q_ref[:, ks].astype(F32)
            vh_b = v_ref[:, vs]
            a_far = jnp.where((row - col) >= sub, factored_blocks(h, 1, 0.0), 0.0)
            o = _dot(a_far.astype(BF16), vh_b)
            k_pad[sub:sub + n, :] = k_ref[:, ks].astype(F32)
            c_pad[sub:sub + n, :] = cum
            v_pad[sub:sub + n, :] = vh_b.astype(F32)
            for d in range(sub):
                kd = k_pad[sub - d:sub - d + n, :]
                cd = c_pad[sub - d:sub - d + n, :]
                term_buf[d * n:(d + 1) * n, :] = (qh * kd * jnp.exp2(cum - cd)).astype(BF16)
            near = _dot(term_buf[...], ones)
            for d in range(sub):
                o = o + near[d * n:(d + 1) * n, :] * v_pad[sub - d:sub - d + n, :]
            intra_buf[:, vs] = o

    for h, (ks, vs) in enumerate(heads):
        cum = cums[h]
        last = cum[n - 1:n, :]
        qh = q_ref[:, ks].astype(F32)
        kh = k_ref[:, ks].astype(F32)
        vh_b = v_ref[:, vs]
        st = state_ref[h]
        qd = (qh * jnp.exp2(cum)).astype(BF16)
        if fast:
            kd = (kh * jnp.exp2(-cum)).astype(BF16)
            a = jnp.where(row >= col, _dot_nt(qd, kd), 0.0)
            intra = _dot(a.astype(BF16), vh_b)
        else:
            intra = intra_buf[:, vs]
        o = intra + _dot_nt(qd, st.astype(BF16))
        state_ref[h] = st * jnp.exp2(last) + _dot_tn(vh_b, (kh * jnp.exp2(last - cum)).astype(BF16))
        o_ref[:, vs] = (_rms(o, nw_ref[...]) * r_ref[:, vs].astype(F32)).astype(o_ref.dtype)


def _gla_scan(q, k, v, logg, r, norm_w, batch, chunks):
    t = q.shape[0]
    n = GLA_CHUNK
    nc = t // batch // (n * chunks)
    tri = np.tile(np.tril(np.ones((n, n), np.float32)), (1, SPLIT_PIECES))
    row = lambda w: pl.BlockSpec((n * chunks, w), lambda b, c: (b * nc + c, 0))
    return pl.pallas_call(
        functools.partial(_gla_kernel, chunks=chunks),
        grid=(batch, nc),
        in_specs=[row(GLA_KEY), row(GLA_KEY), row(GLA_VAL), row(GLA_KEY), row(GLA_VAL),
                  _resident((1, GLA_HV)), _resident((n, SPLIT_PIECES * n))],
        out_specs=row(GLA_VAL),
        out_shape=jax.ShapeDtypeStruct((t, GLA_VAL), BF16),
        scratch_shapes=[pltpu.VMEM((GLA_HEADS, GLA_HV, GLA_HK), F32),
                        pltpu.VMEM((n + GLA_SUB, GLA_HK), F32),
                        pltpu.VMEM((n + GLA_SUB, GLA_HK), F32),
                        pltpu.VMEM((n + GLA_SUB, GLA_HV), F32),
                        pltpu.VMEM((GLA_SUB * n, GLA_HK), BF16),
                        pltpu.VMEM((n, GLA_VAL), F32)],
        compiler_params=_params(("arbitrary", "arbitrary")),
        name="gla_scan",
    )(q, k, v, logg, r, norm_w[None, :], jnp.asarray(tri, BF16))


def _proj_residual_kernel(*refs, widths):
    x_ref = refs[0]
    a_refs = refs[1:1 + len(widths)]
    w_ref = refs[1 + len(widths)]
    o_ref = refs[2 + len(widths)]
    for c0 in range(0, D_MODEL, MATMUL_COLS):
        cs = slice(c0, c0 + MATMUL_COLS)
        acc = x_ref[:, cs]
        off = 0
        for a_ref, width in zip(a_refs, widths):
            acc = acc + _dot(a_ref[...], w_ref[off:off + width, cs])
            off += width
        o_ref[:, cs] = acc


def _proj_residual(x, acts, w, tile):
    t = x.shape[0]
    tm = min(tile, t)
    widths = tuple(a.shape[1] for a in acts)
    row = lambda wd: pl.BlockSpec((tm, wd), lambda i: (i, 0))
    return pl.pallas_call(
        functools.partial(_proj_residual_kernel, widths=widths),
        grid=(t // tm,),
        in_specs=[row(D_MODEL)] + [row(wd) for wd in widths] + [_resident(w.shape)],
        out_specs=row(D_MODEL),
        out_shape=jax.ShapeDtypeStruct((t, D_MODEL), F32),
        compiler_params=_params(("parallel",)),
        name="proj_residual",
    )(x, *acts, w)


def _norm_proj_kernel(x_ref, g_ref, w_ref, *o_refs):
    xn = _rms(x_ref[...], g_ref[...]).astype(BF16)
    off = 0
    for o_ref in o_refs:
        width = o_ref.shape[1]
        _proj_cols(xn, w_ref, off, width, o_ref)
        off += width


def _norm_proj(x, gain, w, widths):
    t = x.shape[0]
    tm = min(TOKEN_TILE, t)
    row = lambda wd: pl.BlockSpec((tm, wd), lambda i: (i, 0))
    return pl.pallas_call(
        _norm_proj_kernel,
        grid=(t // tm,),
        in_specs=[row(D_MODEL), _resident((1, D_MODEL)), _resident(w.shape)],
        out_specs=[row(wd) for wd in widths],
        out_shape=[jax.ShapeDtypeStruct((t, wd), BF16) for wd in widths],
        compiler_params=_params(("parallel",)),
        name="norm_proj",
    )(x, gain, w)


def _qkv_proj_kernel(x_ref, g_ref, wqt_ref, wk_ref, wvt_ref, qt_ref, k_ref, vt_ref):
    xn = _rms(x_ref[...], g_ref[...]).astype(BF16)
    for c0 in range(0, DIFF_QK, MATMUL_COLS):
        cs = slice(c0, c0 + MATMUL_COLS)
        qt_ref[cs, :] = (_dot_nt(wqt_ref[cs, :], xn) * DIFF_Q_SCALE).astype(BF16)
        vt_ref[cs, :] = _dot_nt(wvt_ref[cs, :], xn).astype(BF16)
    _proj_cols(xn, wk_ref, 0, DIFF_QK, k_ref)


def _qkv_proj(x, gain, wq_t, wk, wv_t, tm):
    t = x.shape[0]
    row = lambda wd: pl.BlockSpec((tm, wd), lambda i: (i, 0))
    tr = pl.BlockSpec((None, DIFF_QK, tm), lambda i: (i, 0, 0))
    tr_shape = jax.ShapeDtypeStruct((t // tm, DIFF_QK, tm), BF16)
    return pl.pallas_call(
        _qkv_proj_kernel,
        grid=(t // tm,),
        in_specs=[row(D_MODEL), _resident((1, D_MODEL)), _resident(wq_t.shape), _resident(wk.shape),
                  _resident(wv_t.shape)],
        out_specs=[tr, row(DIFF_QK), tr],
        out_shape=[tr_shape, jax.ShapeDtypeStruct((t, DIFF_QK), BF16), tr_shape],
        compiler_params=_params(("parallel",)),
        name="qkv_proj",
    )(x, gain, wq_t, wk, wv_t)


def _diff_attn_kernel(slope_ref, qt_ref, k_ref, kx_ref, vt_ref, lam_ref, nw_ref, o_ref,
                      acc_ref, m_ref, qa_ref, s_a, s_b, s_c, mx_a, mx_b, mx_c, *, lambda_init):
    nq, _, tile = qt_ref.shape
    h = pl.program_id(1)
    i = pl.program_id(2)
    tile_step = slope_ref[h] * (LOG2E * tile)
    half = DIFF_HEAD_DIM

    def build_queries(j, slot):
        qt = qt_ref[j]
        zeros = jnp.zeros((half, tile), BF16)
        ones_rows = (lax.broadcasted_iota(jnp.int32, (LANES, tile), 0) < ALIBI_PIECES).astype(BF16)
        qa_ref[slot, :, :tile] = jnp.concatenate([qt[:half], zeros, ones_rows], axis=0)
        qa_ref[slot, :, tile:] = jnp.concatenate([zeros, qt[half:], ones_rows], axis=0)

    def scores(t, buf, masked, slot):
        s_ref, mx_ref = buf
        start = pl.multiple_of(t * tile, tile)
        k_aug = jnp.concatenate([k_ref[pl.ds(start, tile), :], kx_ref[...]], axis=1)
        s = _dot(k_aug, qa_ref[slot])
        if masked:
            key_pos = lax.broadcasted_iota(jnp.int32, (tile, 2 * tile), 0)
            lane_pos = lax.broadcasted_iota(jnp.int32, (tile, 2 * tile), 1)
            causal = key_pos <= jnp.where(lane_pos >= tile, lane_pos - tile, lane_pos)
            s = jnp.where(causal, s, NEG_BIG)
        s_ref[...] = s
        mx_ref[...] = jnp.max(s.reshape(tile // SUBLANES, SUBLANES, 2 * tile), axis=0)

    def softmax_pv(t, buf):
        s_ref, mx_ref = buf
        off = -tile_step * (i - t).astype(F32)
        m_old = m_ref[0:1, :]
        m_new = jnp.maximum(m_old, jnp.max(mx_ref[...], axis=0, keepdims=True) + off)
        alpha = jnp.exp2(m_old - m_new)
        shift = off - m_new
        v_aug = jnp.concatenate([vt_ref[t], jnp.ones((DENOM_ROWS, tile), BF16)], axis=0)
        p = jnp.exp2(s_ref[...] + shift)
        acc_ref[...] = alpha * acc_ref[...] + _dot(v_aug, p.astype(BF16))
        m_ref[0:1, :] = m_new

    buf_a = (s_a, mx_a)
    buf_b = (s_b, mx_b)
    buf_c = (s_c, mx_c)
    cur = i % 2
    even = cur == 0

    @pl.when(i == 0)
    def _():
        build_queries(0, 0)
        scores(0, buf_a, True, 0)

    acc_ref[...] = jnp.zeros_like(acc_ref)
    m_ref[...] = jnp.full_like(m_ref, NEG_BIG)

    @pl.when(i == 1)
    def _():
        scores(1, buf_a, True, cur)
        softmax_pv(0, buf_c)

    @pl.when(jnp.logical_and(even, i >= 2))
    def _():
        scores(1, buf_b, False, cur)
        softmax_pv(0, buf_c)

    @pl.when(jnp.logical_and(jnp.logical_not(even), i >= 3))
    def _():
        scores(1, buf_a, False, cur)
        softmax_pv(0, buf_c)
        scores(2, buf_b, False, cur)
        softmax_pv(1, buf_a)

    first = jnp.where(even, 1, 2)

    def pair(u, carry):
        t = first + 2 * u
        scores(t + 1, buf_a, False, cur)
        softmax_pv(t, buf_b)
        scores(t + 2, buf_b, False, cur)
        softmax_pv(t + 1, buf_a)
        return carry

    lax.fori_loop(0, (i - 2) // 2, pair, 0)

    @pl.when(i >= 2)
    def _():
        scores(i, buf_a, True, cur)
        softmax_pv(i - 1, buf_b)

    @pl.when(i + 1 < nq)
    def _():
        build_queries(i + 1, 1 - cur)
        scores(0, buf_c, False, 1 - cur)
        softmax_pv(i, buf_a)

    @pl.when(i + 1 == nq)
    def _():
        softmax_pv(i, buf_a)

    lam = (jnp.exp(jnp.sum(lam_ref[0:1, :] * lam_ref[1:2, :], axis=-1, keepdims=True))
           - jnp.exp(jnp.sum(lam_ref[2:3, :] * lam_ref[3:4, :], axis=-1, keepdims=True)) + lambda_init)
    normed = acc_ref[0:DIFF_V_DIM, :] / acc_ref[DIFF_V_DIM:DIFF_V_DIM + 1, :]
    o = (normed[:, :tile] - lam * normed[:, tile:]).T
    o_ref[...] = (_rms(o, nw_ref[...]) * (1.0 - lambda_init)).astype(o_ref.dtype)


def _alibi_slopes(n):
    start = 2.0 ** (-8.0 / n)
    return np.array([start ** (i + 1) for i in range(n)], dtype=np.float32)


def _alibi_key_columns(tile):
    target = (LOG2E * _alibi_slopes(DIFF_HEADS))[:, None] * np.arange(tile, dtype=np.float32)[None, :]
    rest = jnp.asarray(target, F32)
    pieces = []
    for _ in range(ALIBI_PIECES):
        piece = rest.astype(BF16)
        pieces.append(piece)
        rest = rest - piece.astype(F32)
    cols = jnp.stack(pieces, axis=-1)
    return jnp.pad(cols, ((0, 0), (0, 0), (0, LANES - ALIBI_PIECES)))


def _diff_attention(qt, k, vt, lam_vecs, subln, lambda_init, batch):
    nblk, _, tile = qt.shape
    nq = nblk // batch
    seq = nq * tile
    return pl.pallas_call(
        functools.partial(_diff_attn_kernel, lambda_init=lambda_init),
        grid=(batch, DIFF_HEADS, nq),
        in_specs=[pl.BlockSpec(memory_space=pltpu.SMEM),
                  pl.BlockSpec((nq, LANES, tile), lambda b, h, i: (b, h, 0), pipeline_mode=pl.Buffered(1)),
                  pl.BlockSpec((seq, LANES), lambda b, h, i: (b, h), pipeline_mode=pl.Buffered(1)),
                  pl.BlockSpec((None, tile, LANES), lambda b, h, i: (h, 0, 0)),
                  pl.BlockSpec((nq, LANES, tile), lambda b, h, i: (b, h, 0), pipeline_mode=pl.Buffered(1)),
                  _resident((4, DIFF_HEAD_DIM)), _resident((1, DIFF_V_DIM))],
        out_specs=pl.BlockSpec((tile, LANES), lambda b, h, i: (b * nq + i, h)),
        out_shape=jax.ShapeDtypeStruct((batch * seq, DIFF_HEADS * DIFF_V_DIM), BF16),
        scratch_shapes=[pltpu.VMEM((DIFF_V_DIM + DENOM_ROWS, 2 * tile), F32), pltpu.VMEM((SUBLANES, 2 * tile), F32),
                        pltpu.VMEM((2, 2 * LANES, 2 * tile), BF16)]
                       + [pltpu.VMEM((tile, 2 * tile), F32)] * 3 + [pltpu.VMEM((SUBLANES, 2 * tile), F32)] * 3,
        compiler_params=_params(("arbitrary", "arbitrary", "arbitrary")),
        name="diff_attention",
    )(jnp.asarray(_alibi_slopes(DIFF_HEADS)), qt, k, _alibi_key_columns(tile), vt, lam_vecs, subln[None, :])


def _cross_attn_kernel(x_ref, g_ref, wq_ref, k_ref, v_ref, wo_ref, o_ref, q_buf, att_buf):
    heads = [slice(h * X_HEAD_DIM, (h + 1) * X_HEAD_DIM) for h in range(X_HEADS)]
    xn = _rms(x_ref[...], g_ref[...]).astype(BF16)
    for c0 in range(0, D_MODEL, MATMUL_COLS):
        cs = slice(c0, c0 + MATMUL_COLS)
        q_buf[:, cs] = _dot(xn, wq_ref[:, cs]).astype(BF16)
    scores = [_dot_nt(q_buf[:, hs], k_ref[:, hs]) * (X_HEAD_DIM ** -0.5) for hs in heads]
    for hs, s in zip(heads, scores):
        p = jnp.exp(s - jnp.max(s, axis=-1, keepdims=True))
        pv = _dot(p.astype(BF16), v_ref[:, hs])
        att_buf[:, hs] = (pv / jnp.sum(p, axis=-1, keepdims=True)).astype(BF16)
    for c0 in range(0, D_MODEL, MATMUL_COLS):
        cs = slice(c0, c0 + MATMUL_COLS)
        o_ref[:, cs] = x_ref[:, cs] + _dot(att_buf[...], wo_ref[:, cs])


def _cross_attention(x, gain, wq, k_mem, v_mem, wo, batch):
    t = x.shape[0]
    seq = t // batch
    tm = min(CROSS_TILE, seq)
    per = seq // tm
    n_mem = k_mem.shape[0] // batch
    row = pl.BlockSpec((tm, D_MODEL), lambda i: (i, 0))
    mem = pl.BlockSpec((n_mem, D_MODEL), lambda i: (i // per, 0))
    return pl.pallas_call(
        _cross_attn_kernel,
        grid=(t // tm,),
        in_specs=[row, _resident((1, D_MODEL)), _resident(wq.shape), mem, mem, _resident(wo.shape)],
        out_specs=row,
        out_shape=jax.ShapeDtypeStruct((t, D_MODEL), F32),
        scratch_shapes=[pltpu.VMEM((tm, D_MODEL), BF16), pltpu.VMEM((tm, D_MODEL), BF16)],
        compiler_params=_params(("parallel",)),
        name="cross_attention",
    )(x, gain, wq, k_mem, v_mem, wo)


def _mlp_kernel(x_ref, g_ref, w1_ref, w2_ref, fg_ref, o_ref, acc_ref, *, final_norm):
    x = x_ref[...]
    xn = _rms(x, g_ref[...]).astype(BF16)
    acc_ref[...] = x
    for c0 in range(0, D_FF, MATMUL_COLS):
        hid = jnp.maximum(_dot(xn, w1_ref[:, c0:c0 + MATMUL_COLS]), 0.0)
        acc_ref[...] += _dot((hid * hid).astype(BF16), w2_ref[c0:c0 + MATMUL_COLS, :])
    out = acc_ref[...]
    if final_norm:
        out = _rms(out, fg_ref[...])
    o_ref[...] = out


def _mlp(x, gain, w1, w2, final_gain, final_norm, tile):
    t = x.shape[0]
    tm = min(tile, t)
    row = pl.BlockSpec((tm, D_MODEL), lambda i: (i, 0))
    return pl.pallas_call(
        functools.partial(_mlp_kernel, final_norm=final_norm),
        grid=(t // tm,),
        in_specs=[row, _resident((1, D_MODEL)), _resident(w1.shape), _resident(w2.shape), _resident((1, D_MODEL))],
        out_specs=row,
        out_shape=jax.ShapeDtypeStruct((t, D_MODEL), F32),
        scratch_shapes=[pltpu.VMEM((tm, D_MODEL), F32)],
        compiler_params=_params(("parallel",)),
        name="mlp",
    )(x, gain, w1, w2, final_gain)


def _even_weights(w_in, gla_w2):
    z_end = SSD_INNER
    xbc_end = z_end + SSD_XBC
    dt_end = xbc_end + SSD_HEADS
    v_end = dt_end + 2 * GLA_KEY + GLA_VAL
    glr_end = v_end + GLA_RANK
    w_main = jnp.concatenate([w_in[:, :xbc_end], w_in[:, dt_end:v_end], w_in[:, glr_end:]], axis=1).astype(BF16)
    w_small = jnp.concatenate([w_in[:, xbc_end:dt_end], w_in[:, v_end:glr_end],
                               jnp.zeros((D_MODEL, LANES - SSD_HEADS - GLA_RANK), w_in.dtype)], axis=1).astype(BF16)
    w2_pad = jnp.zeros((LANES, GLA_KEY), F32).at[SSD_HEADS:SSD_HEADS + GLA_RANK].set(gla_w2).astype(BF16)
    return w_main, w_small, w2_pad


def kernel(x, mem, ev_norm, ev_w_in, ev_conv_w, ev_conv_b, ev_dt_bias, ev_a_log, ev_d_skip, ev_ssd_norm, ev_gla_w2, ev_gla_b, ev_gla_norm, ev_w_out, od_norm, od_w_qkv, od_lam_q1, od_lam_k1, od_lam_q2, od_lam_k2, od_subln, od_w_o, xa_norm, xa_mem_norm, xa_wq, xa_wkv, xa_wo, mlp_norm, mlp_w1, mlp_w2, final_norm):
    batch, seq, d = x.shape
    n_mem = mem.shape[1]
    n_layers = xa_norm.shape[0]
    xf = x.reshape(batch * seq, d)
    memf = mem.reshape(batch * n_mem, d)
    for layer in range(n_layers):
        i = layer // 2
        tail_tile = 2 * TOKEN_TILE
        if layer % 2 == 0:
            w_main, w_small, w2_pad = _even_weights(ev_w_in[i], ev_gla_w2[i])
            zg, u, q, k, v, rg, small, logg = _even_in_proj(xf, ev_norm[i][None, :], w_main, w_small, w2_pad,
                                                            ev_gla_b[i][None, :], ev_conv_w[i], ev_conv_b[i], seq)
            y = _ssd_scan(u, zg, small, ev_dt_bias[i], ev_a_log[i], ev_d_skip[i], ev_ssd_norm[i], batch, SSD_CHUNKS_PER_STEP)
            o = _gla_scan(q, k, v, logg, rg, ev_gla_norm[i], batch, GLA_CHUNKS_PER_STEP)
            xf = _proj_residual(xf, [y, o], ev_w_out[i].astype(BF16), tail_tile)
        else:
            lambda_init = 0.8 - 0.6 * math.exp(-0.3 * layer)
            w_qkv = od_w_qkv[i].astype(BF16)
            qt, k, vt = _qkv_proj(xf, od_norm[i][None, :], w_qkv[:, :DIFF_QK].T, w_qkv[:, DIFF_QK:2 * DIFF_QK],
                                  w_qkv[:, 2 * DIFF_QK:].T, min(ATTN_TILE, seq))
            lam_vecs = jnp.stack([od_lam_q1[i], od_lam_k1[i], od_lam_q2[i], od_lam_k2[i]]).astype(F32)
            att = _diff_attention(qt, k, vt, lam_vecs, od_subln[i], lambda_init, batch)
            xf = _proj_residual(xf, [att], od_w_o[i].astype(BF16), tail_tile)
        k_mem, v_mem = _norm_proj(memf, xa_mem_norm[layer][None, :], xa_wkv[layer].astype(BF16), (d, d))
        xf = _cross_attention(xf, xa_norm[layer][None, :], xa_wq[layer].astype(BF16), k_mem, v_mem,
                              xa_wo[layer].astype(BF16), batch)
        xf = _mlp(xf, mlp_norm[layer][None, :], mlp_w1[layer].astype(BF16), mlp_w2[layer].astype(BF16),
                  final_norm[None, :], layer == n_layers - 1, tail_tile)
    return xf.reshape(batch, seq, d)
```
